```python
import math
import jax, jax.numpy as jnp
from jax import lax
import numpy as np

D_MODEL = 1024
BATCH = 8
SEQ = 2048
DEPTH = 4
DEC_BATCH = 32
DEC_SEQ = 8
PAST_LEN = 8192
PAGE_SIZE = 128

N_MIXERS = 4
D_FF = 4 * D_MODEL
EPS = 1e-6

N_MEM = 256
XA_HEADS = 4
XA_DIM = 64
XA_WIDTH = XA_HEADS * XA_DIM

GDN_DK = 128
GDN_DV = 128
GDN_HEADS = D_MODEL // GDN_DV
GDN_KEY = GDN_HEADS * GDN_DK
GDN_VAL = GDN_HEADS * GDN_DV
GDN_CONV_CH = 2 * GDN_KEY + GDN_VAL
CONV_W = 4
GDN_CHUNK = 64
GDN_IN = GDN_CONV_CH + GDN_VAL + 2 * GDN_HEADS

HG_DK = 128
HG_DV = 128
HG_HEADS = D_MODEL // HG_DK
HG_WIDTH = HG_HEADS * HG_DK
HG_CHUNK = 32
HG_IN = 4 * HG_WIDTH

FOX_DIM = 64
FOX_HEADS = D_MODEL // FOX_DIM
FOX_WIDTH = FOX_HEADS * FOX_DIM
Q_BLOCK = 128
FOX_IN = 4 * FOX_WIDTH + FOX_HEADS
FOX_FBIAS_LO = 2.0
FOX_FBIAS_HI = 9.0

CM_CHUNK = 128
CM_GROUPS = 8
CM_WIDTH = D_MODEL
CM_GDIM = CM_WIDTH // CM_GROUPS
CM_IN = 2 * CM_WIDTH

MIX_OUT = D_MODEL + XA_WIDTH
N_A = (DEPTH + N_MIXERS - 1) // N_MIXERS
N_B = (DEPTH - 1 + N_MIXERS - 1) // N_MIXERS
N_C = (DEPTH - 2 + N_MIXERS - 1) // N_MIXERS
N_D = (DEPTH - 3 + N_MIXERS - 1) // N_MIXERS

kernel_name = 'hybrid_gdn_hgrn2_fox_chunkmlp_decode_step'


def rmsnorm(x, g):
    xf = x.astype(jnp.float32)
    y = xf * lax.rsqrt(jnp.mean(xf * xf, -1, keepdims=True) + EPS)
    return (y * g.astype(jnp.float32)).astype(x.dtype)


def layernorm(x, g, b):
    xf = x.astype(jnp.float32)
    mu = jnp.mean(xf, -1, keepdims=True)
    xc = xf - mu
    y = xc * lax.rsqrt(jnp.mean(xc * xc, -1, keepdims=True) + EPS)
    return (y * g.astype(jnp.float32) + b.astype(jnp.float32)).astype(x.dtype)


def l2norm(x):
    xf = x.astype(jnp.float32)
    return xf * lax.rsqrt(jnp.sum(xf * xf, -1, keepdims=True) + EPS)


def _to_chunks(a, c):
    b, t = a.shape[:2]
    pad = (-t) % c
    a = jnp.pad(a, [(0, 0), (0, pad)] + [(0, 0)] * (a.ndim - 2))
    a = a.reshape((b, (t + pad) // c, c) + a.shape[2:])
    return jnp.moveaxis(jnp.moveaxis(a, 3, 2), 1, 0)


def _from_chunks(o, t):
    n, b, h, c, d = o.shape
    return jnp.moveaxis(o, 0, 1).transpose(0, 1, 3, 2, 4).reshape(b, n * c, h, d)[:, :t]


def gated_delta_rule(q, k, v, g, beta, s0):
    f32 = jnp.float32
    t = q.shape[1]
    c = GDN_CHUNK
    dv = v.shape[-1]
    q = q.astype(f32) * (q.shape[-1] ** -0.5)
    xs = tuple(_to_chunks(a.astype(f32), c) for a in (q, k, v, g, beta))
    incl = jnp.tril(jnp.ones((c, c), dtype=bool))
    strict = jnp.tril(jnp.ones((c, c), dtype=bool), -1)
    eye = jnp.eye(c, dtype=f32)

    def step(S, inp):
        qc, kc, vc, gc, bc = inp
        G = jnp.cumsum(gc, -1)
        decay = jnp.exp(jnp.where(incl, G[..., :, None] - G[..., None, :], -jnp.inf))
        a = jnp.where(strict, bc[..., :, None] * jnp.einsum('bhtd,bhsd->bhts', kc, kc) * decay, 0.0)
        rhs = jnp.concatenate([vc * bc[..., None], kc * (bc * jnp.exp(G))[..., None]], -1)
        sol = lax.linalg.triangular_solve(eye + a, rhs, left_side=True, lower=True)
        u = sol[..., :dv] - jnp.einsum('bhtk,bhkv->bhtv', sol[..., dv:], S)
        att = jnp.einsum('bhtd,bhsd->bhts', qc, kc) * decay
        o = jnp.einsum('bhtk,bhkv->bhtv', qc * jnp.exp(G)[..., None], S) + jnp.einsum('bhts,bhsv->bhtv', att, u)
        gl = G[..., -1:]
        S = S * jnp.exp(gl)[..., None] + jnp.einsum('bhtk,bhtv->bhkv', kc * jnp.exp(gl - G)[..., None], u)
        return S, o

    S, o = lax.scan(step, s0.astype(f32), xs)
    return _from_chunks(o, t), S


def gla_recurrence(q, k, v, logf, s0):
    f32 = jnp.float32
    t = q.shape[1]
    c = HG_CHUNK
    xs = tuple(_to_chunks(a.astype(f32), c) for a in (q, k, v, logf))
    incl = jnp.tril(jnp.ones((c, c), dtype=bool))

    def step(S, inp):
        qc, kc, vc, lc = inp
        Bc = jnp.cumsum(lc, 2)
        w = jnp.exp(jnp.where(incl[:, :, None], Bc[:, :, :, None, :] - Bc[:, :, None, :, :], -jnp.inf))
        att = jnp.einsum('bhtd,bhsd,bhtsd->bhts', qc, kc, w)
        o = jnp.einsum('bhtd,bhdv->bhtv', qc * jnp.exp(Bc), S) + jnp.einsum('bhts,bhsv->bhtv', att, vc)
        bl = Bc[:, :, -1:, :]
        S = S * jnp.exp(bl[:, :, 0, :, None]) + jnp.einsum('bhsd,bhsv->bhdv', kc * jnp.exp(bl - Bc), vc)
        return S, o

    S, o = lax.scan(step, s0.astype(f32), xs)
    return _from_chunks(o, t), S


def gdn_mixer(p, conv_buf, s0, conv_w, a_log, dt_bias, norm_w):
    f32 = jnp.float32
    b, t, _ = p.shape
    qkv, z, beta_raw, a_raw = jnp.split(p, [GDN_CONV_CH, GDN_CONV_CH + GDN_VAL, GDN_CONV_CH + GDN_VAL + GDN_HEADS], -1)
    xx = jnp.concatenate([conv_buf.astype(qkv.dtype), qkv], 1)
    conv = jax.nn.silu(sum(xx[:, i:i + t] * conv_w[i] for i in range(CONV_W)))
    q, k, v = jnp.split(conv, [GDN_KEY, 2 * GDN_KEY], -1)
    q = l2norm(q.reshape(b, t, GDN_HEADS, GDN_DK))
    k = l2norm(k.reshape(b, t, GDN_HEADS, GDN_DK))
    v = v.reshape(b, t, GDN_HEADS, GDN_DV)
    beta = jax.nn.sigmoid(beta_raw.astype(f32))
    g = -jnp.exp(a_log.astype(f32)) * jax.nn.softplus(a_raw.astype(f32) + dt_bias)
    o, s = gated_delta_rule(q, k, v, g, beta, s0)
    o = rmsnorm(o, norm_w) * jax.nn.silu(z.astype(f32).reshape(b, t, GDN_HEADS, GDN_DV))
    return o.reshape(b, t, GDN_VAL).astype(p.dtype), xx[:, t:], s


def hgrn2_mixer(p, s0, lb, norm_w):
    f32 = jnp.float32
    b, t, _ = p.shape
    q, f, i, g = jnp.split(p, 4, axis=-1)
    shp = (b, t, HG_HEADS, HG_DK)
    ff = f.astype(f32).reshape(shp)
    lbh = lb.reshape(HG_HEADS, HG_DK)
    logf = jnp.logaddexp(jnp.log(lbh), jnp.log1p(-lbh) + jax.nn.log_sigmoid(ff))
    k = (1.0 - lbh) * jax.nn.sigmoid(-ff)
    qf = jax.nn.silu(q.astype(f32)).reshape(shp) * HG_DK ** -0.5
    o, s = gla_recurrence(qf, k, i.reshape(b, t, HG_HEADS, HG_DV), logf, s0)
    o = rmsnorm(o, norm_w) * jax.nn.silu(g.astype(f32).reshape(b, t, HG_HEADS, HG_DV))
    return o.reshape(b, t, HG_WIDTH).astype(p.dtype), s


def fox_project(p, fbias, qn, kn):
    b, t, _ = p.shape
    q, k, v, og, fl = jnp.split(p, [FOX_WIDTH, 2 * FOX_WIDTH, 3 * FOX_WIDTH, 4 * FOX_WIDTH], -1)
    shp = (b, t, FOX_HEADS, FOX_DIM)
    q = rmsnorm(q.reshape(shp), qn)
    k = rmsnorm(k.reshape(shp), kn)
    logf = jax.nn.log_sigmoid(fl.astype(jnp.float32) + fbias)
    return q, k, v.reshape(shp), logf, jax.nn.sigmoid(og)


def fox_prompt(q, k, v, logf):
    f32 = jnp.float32
    b, t, h, d = q.shape
    nb = t // Q_BLOCK
    F = jnp.cumsum(logf, 1)
    Fk = F.transpose(0, 2, 1)
    pos = jnp.arange(t)
    qb = q.reshape(b, nb, Q_BLOCK, h, d).swapaxes(0, 1)
    fb = F.reshape(b, nb, Q_BLOCK, h).swapaxes(0, 1)
    pb = pos.reshape(nb, Q_BLOCK)

    def one(args):
        qi, fi, pi = args
        s = jnp.einsum('bqhd,bkhd->bhqk', qi, k).astype(f32) * FOX_DIM ** -0.5
        s = s + fi.transpose(0, 2, 1)[..., :, None] - Fk[:, :, None, :]
        s = jnp.where(pos[None, :] <= pi[:, None], s, -jnp.inf)
        pr = jax.nn.softmax(s, -1).astype(v.dtype)
        return jnp.einsum('bhqk,bkhd->bqhd', pr, v)

    o = lax.map(one, (qb, fb, pb))
    return o.swapaxes(0, 1).reshape(b, t, h, d)


def fox_sample(q, k, v, logf, k_past, v_past, logf_past):
    f32 = jnp.float32
    p = k_past.shape[1]
    t = q.shape[1]
    scale = FOX_DIM ** -0.5
    fn = jnp.cumsum(logf, 1).transpose(0, 2, 1)
    rc = lax.cumsum(logf_past.astype(f32), axis=1, reverse=True)
    rest = jnp.concatenate([rc[:, 1:], jnp.zeros_like(rc[:, :1])], 1).transpose(0, 2, 1)
    s_past = jnp.einsum('bqhd,bkhd->bhqk', q, k_past).astype(f32) * scale + fn[..., :, None] + rest[..., None, :]
    s_new = jnp.einsum('bqhd,bkhd->bhqk', q, k).astype(f32) * scale + fn[..., :, None] - fn[..., None, :]
    s_new = jnp.where(jnp.tril(jnp.ones((t, t), dtype=bool)), s_new, -jnp.inf)
    pr = jax.nn.softmax(jnp.concatenate([s_past, s_new], -1), -1).astype(v.dtype)
    return jnp.einsum('bhqk,bkhd->bqhd', pr[..., :p], v_past) + jnp.einsum('bhqk,bkhd->bqhd', pr[..., p:], v)


def chunk_mlp_mixer(p, ln_g, ln_b, ws, bs):
    b, t, _ = p.shape
    z = jax.nn.gelu(p)
    u, v = jnp.split(z, 2, -1)
    v = layernorm(v, ln_g, ln_b)
    pad = (-t) % CM_CHUNK
    n = (t + pad) // CM_CHUNK
    vc = jnp.pad(v, ((0, 0), (0, pad), (0, 0))).reshape(b, n, CM_CHUNK, CM_GROUPS, CM_GDIM)
    wm = jnp.where(jnp.tril(jnp.ones((CM_CHUNK, CM_CHUNK), dtype=bool)), ws, 0.0).astype(v.dtype)
    mixed = jnp.einsum('grc,bncgd->bnrgd', wm, vc) + bs.T[:, :, None].astype(v.dtype)
    mixed = mixed.reshape(b, n * CM_CHUNK, CM_WIDTH)[:, :t]
    return u * mixed, v


def memory_kv(mem, g, w_kv, kn):
    b, n, _ = mem.shape
    k, v = jnp.split(rmsnorm(mem, g) @ w_kv, 2, -1)
    shp = (b, n, XA_HEADS, XA_DIM)
    return rmsnorm(k.reshape(shp), kn), v.reshape(shp)


def memory_attend(q, mk, mv, qn):
    b, t, _ = q.shape
    q = rmsnorm(q.reshape(b, t, XA_HEADS, XA_DIM), qn)
    s = jnp.einsum('bthd,bnhd->bhtn', q, mk).astype(jnp.float32) * XA_DIM ** -0.5
    pr = jax.nn.softmax(s, -1).astype(mv.dtype)
    return jnp.einsum('bhtn,bnhd->bthd', pr, mv).reshape(b, t, XA_WIDTH)


def sq_relu_mlp(x, w_up, w_down):
    return jnp.square(jax.nn.relu(x @ w_up)) @ w_down


def setup_inputs(seed: int = 0) -> dict:
    key = jax.random.key(seed)
    ks = iter(jax.random.split(key, 48))

    def nrm(shape, scale=1.0):
        return jax.random.normal(next(ks), shape, jnp.float32) * scale

    def gain(shape):
        return 1.0 + nrm(shape, 0.05)

    D = D_MODEL
    n_pages = PAST_LEN // PAGE_SIZE
    n_used = DEC_BATCH * n_pages
    n_pool = n_used + max(1, n_used // 4)
    cache_fbias = jax.random.uniform(next(ks), (N_C, 1, 1, FOX_HEADS), jnp.float32, FOX_FBIAS_LO, FOX_FBIAS_HI)
    cache_logf = jax.nn.log_sigmoid(nrm((N_C, n_pool, PAGE_SIZE, FOX_HEADS), 0.5) + cache_fbias)
    return {
        'x_prompt': nrm((BATCH, SEQ, D)),
        'x_sample': nrm((DEC_BATCH, DEC_SEQ, D)),
        'mem_prompt': nrm((BATCH, N_MEM, D)),
        'state_a_conv': nrm((N_A, DEC_BATCH, CONV_W - 1, GDN_CONV_CH)),
        'state_a_ssm': nrm((N_A, DEC_BATCH, GDN_HEADS, GDN_DK, GDN_DV), 0.1),
        'state_b_ssm': nrm((N_B, DEC_BATCH, HG_HEADS, HG_DK, HG_DV), 0.3),
        'cache_c_k': nrm((N_C, n_pool, PAGE_SIZE, FOX_HEADS, FOX_DIM)),
        'cache_c_v': nrm((N_C, n_pool, PAGE_SIZE, FOX_HEADS, FOX_DIM)),
        'cache_c_logf': cache_logf,
        'cache_mem_k': nrm((DEPTH, DEC_BATCH, N_MEM, XA_HEADS, XA_DIM)),
        'cache_mem_v': nrm((DEPTH, DEC_BATCH, N_MEM, XA_HEADS, XA_DIM)),
        'page_table': jax.random.permutation(next(ks), n_pool)[:n_used].reshape(DEC_BATCH, n_pages).astype(jnp.int32),
        'norm_mix': gain((DEPTH, D)),
        'w_out': nrm((DEPTH, MIX_OUT, D), MIX_OUT ** -0.5),
        'norm_mlp': gain((DEPTH, D)),
        'w_up': nrm((DEPTH, D, D_FF), D ** -0.5),
        'w_down': nrm((DEPTH, D_FF, D), D_FF ** -0.5),
        'mem_norm': gain((DEPTH, D)),
        'w_mem_kv': nrm((DEPTH, D, 2 * XA_WIDTH), D ** -0.5),
        'xa_qnorm': gain((DEPTH, XA_DIM)),
        'xa_knorm': gain((DEPTH, XA_DIM)),
        'w_in_a': nrm((N_A, D, GDN_IN + XA_WIDTH), D ** -0.5),
        'a_conv_w': nrm((N_A, CONV_W, GDN_CONV_CH), CONV_W ** -0.5),
        'a_log': jnp.log(jax.random.uniform(next(ks), (N_A, GDN_HEADS), jnp.float32, 1.0, 16.0)),
        'a_dt_bias': (lambda dt: dt + jnp.log(-jnp.expm1(-dt)))(jnp.exp(jax.random.uniform(next(ks), (N_A, GDN_HEADS), jnp.float32, math.log(1e-3), math.log(1e-1)))),
        'a_norm_w': gain((N_A, GDN_DV)),
        'w_in_b': nrm((N_B, D, HG_IN + XA_WIDTH), D ** -0.5),
        'hg_lb': nrm((DEPTH, HG_WIDTH), 0.5),
        'b_norm_w': gain((N_B, HG_DV)),
        'w_in_c': nrm((N_C, D, FOX_IN + XA_WIDTH), D ** -0.5),
        'c_fbias': jax.random.uniform(next(ks), (N_C, FOX_HEADS), jnp.float32, FOX_FBIAS_LO, FOX_FBIAS_HI),
        'c_qnorm': gain((N_C, FOX_DIM)),
        'c_knorm': gain((N_C, FOX_DIM)),
        'w_in_d': nrm((N_D, D, CM_IN + XA_WIDTH), D ** -0.5),
        'd_ln_g': gain((N_D, CM_WIDTH)),
        'd_ln_b': nrm((N_D, CM_WIDTH), 0.02),
        'd_ws': nrm((N_D, CM_GROUPS, CM_CHUNK, CM_CHUNK), CM_CHUNK ** -0.5),
        'd_bs': 1.0 + nrm((N_D, CM_GROUPS, CM_CHUNK), 0.05),
    }


def reference(x_prompt, x_sample, mem_prompt, state_a_conv, state_a_ssm, state_b_ssm,
              cache_c_k, cache_c_v, cache_c_logf, cache_mem_k, cache_mem_v, page_table,
              norm_mix, w_out, norm_mlp, w_up, w_down, mem_norm, w_mem_kv, xa_qnorm, xa_knorm,
              w_in_a, a_conv_w, a_log, a_dt_bias, a_norm_w,
              w_in_b, hg_lb, b_norm_w,
              w_in_c, c_fbias, c_qnorm, c_knorm,
              w_in_d, d_ln_g, d_ln_b, d_ws, d_bs):
    f32 = jnp.float32
    bp = x_prompt.shape[0]
    ds = x_sample.shape[0]
    past = page_table.shape[1] * PAGE_SIZE
    lb_w = jax.nn.softmax(hg_lb.astype(f32), axis=0)
    lower_bounds = jnp.cumsum(lb_w, axis=0) - lb_w[0]
    a_conv_p, a_conv_s, a_ssm_p, a_ssm_s = [], [], [], []
    b_ssm_p, b_ssm_s = [], []
    c_k_p, c_v_p, c_lf_p, c_k_s, c_v_s, c_lf_s = [], [], [], [], [], []
    d_v_s, mem_k_p, mem_v_p = [], [], []
    hp, hs = x_prompt, x_sample
    for l in range(DEPTH):
        kind, j = l % N_MIXERS, l // N_MIXERS
        w_in = (w_in_a, w_in_b, w_in_c, w_in_d)[kind][j]
        pp = rmsnorm(hp, norm_mix[l]) @ w_in
        ps = rmsnorm(hs, norm_mix[l]) @ w_in
        mix_p, xq_p = pp[..., :-XA_WIDTH], pp[..., -XA_WIDTH:]
        mix_s, xq_s = ps[..., :-XA_WIDTH], ps[..., -XA_WIDTH:]
        mk, mv = memory_kv(mem_prompt, mem_norm[l], w_mem_kv[l], xa_knorm[l])
        mem_k_p.append(mk)
        mem_v_p.append(mv)
        xo_p = memory_attend(xq_p, mk, mv, xa_qnorm[l])
        xo_s = memory_attend(xq_s, cache_mem_k[l], cache_mem_v[l], xa_qnorm[l])
        if kind == 0:
            mo_p, buf, st = gdn_mixer(mix_p, jnp.zeros((bp, CONV_W - 1, GDN_CONV_CH), mix_p.dtype),
                                      jnp.zeros((bp, GDN_HEADS, GDN_DK, GDN_DV), f32),
                                      a_conv_w[j], a_log[j], a_dt_bias[j], a_norm_w[j])
            a_conv_p.append(buf)
            a_ssm_p.append(st)
            mo_s, buf, st = gdn_mixer(mix_s, state_a_conv[j], state_a_ssm[j],
                                      a_conv_w[j], a_log[j], a_dt_bias[j], a_norm_w[j])
            a_conv_s.append(buf)
            a_ssm_s.append(st)
        elif kind == 1:
            mo_p, st = hgrn2_mixer(mix_p, jnp.zeros((bp, HG_HEADS, HG_DK, HG_DV), f32), lower_bounds[l], b_norm_w[j])
            b_ssm_p.append(st)
            mo_s, st = hgrn2_mixer(mix_s, state_b_ssm[j], lower_bounds[l], b_norm_w[j])
            b_ssm_s.append(st)
        elif kind == 2:
            q, k, v, lf, og = fox_project(mix_p, c_fbias[j], c_qnorm[j], c_knorm[j])
            mo_p = fox_prompt(q, k, v, lf).reshape(og.shape) * og
            c_k_p.append(k)
            c_v_p.append(v)
            c_lf_p.append(lf)
            q, k, v, lf, og = fox_project(mix_s, c_fbias[j], c_qnorm[j], c_knorm[j])
            k_past = cache_c_k[j][page_table].reshape(ds, past, FOX_HEADS, FOX_DIM)
            v_past = cache_c_v[j][page_table].reshape(ds, past, FOX_HEADS, FOX_DIM)
            lf_past = cache_c_logf[j][page_table].reshape(ds, past, FOX_HEADS)
            mo_s = fox_sample(q, k, v, lf, k_past, v_past, lf_past).reshape(og.shape) * og
            c_k_s.append(k)
            c_v_s.append(v)
            c_lf_s.append(lf)
        else:
            mo_p, _ = chunk_mlp_mixer(mix_p, d_ln_g[j], d_ln_b[j], d_ws[j], d_bs[j])
            mo_s, vrows = chunk_mlp_mixer(mix_s, d_ln_g[j], d_ln_b[j], d_ws[j], d_bs[j])
            d_v_s.append(vrows)
        hp = hp + jnp.concatenate([mo_p.astype(hp.dtype), xo_p.astype(hp.dtype)], -1) @ w_out[l]
        hs = hs + jnp.concatenate([mo_s.astype(hs.dtype), xo_s.astype(hs.dtype)], -1) @ w_out[l]
        hp = hp + sq_relu_mlp(rmsnorm(hp, norm_mlp[l]), w_up[l], w_down[l])
        hs = hs + sq_relu_mlp(rmsnorm(hs, norm_mlp[l]), w_up[l], w_down[l])
    return (hp, hs,
            jnp.stack(a_conv_p), jnp.stack(a_conv_s), jnp.stack(a_ssm_p), jnp.stack(a_ssm_s),
            jnp.stack(b_ssm_p), jnp.stack(b_ssm_s),
            jnp.stack(c_k_p), jnp.stack(c_v_p), jnp.stack(c_lf_p),
            jnp.stack(c_k_s), jnp.stack(c_v_s), jnp.stack(c_lf_s),
            jnp.stack(d_v_s), jnp.stack(mem_k_p), jnp.stack(mem_v_p))
```

```python
import functools
import math

import jax
import jax.numpy as jnp
from jax import lax
from jax.experimental import pallas as pl
from jax.experimental.pallas import tpu as pltpu

F32 = jnp.float32
BF16 = jnp.bfloat16
EPS = 1e-6

LANES = 128
SUBLANES = 8
VMEM_LIMIT = 56 * 1024 * 1024

XA_HEADS, XA_DIM, N_MEM = 4, 64, 256
XA_WIDTH = XA_HEADS * XA_DIM
HEAD128 = 128
N_HEADS128 = 8
FOX_DIM, FOX_HEADS = 64, 16
PAGE = 128
CM_CHUNK, CM_GROUPS = 128, 8
GDN_CHUNK, HG_CHUNK, HG_SUB = 64, 32, 8
FOX_TQ = 256
PAGES_PER_STEP = 8


def _cparams(sem):
    return pltpu.CompilerParams(dimension_semantics=sem, vmem_limit_bytes=VMEM_LIMIT)


def _iota(shape, dim):
    return lax.broadcasted_iota(jnp.int32, shape, dim)


def _split3(x):
    hi = x.astype(BF16)
    r = x - hi.astype(F32)
    mid = r.astype(BF16)
    lo = (r - mid.astype(F32)).astype(BF16)
    return hi, mid, lo


def _split2(x):
    hi = x.astype(BF16)
    lo = (x - hi.astype(F32)).astype(BF16)
    return hi, lo


_NN = (((1,), (0,)), ((), ()))
_NT = (((1,), (1,)), ((), ()))
_TN = (((0,), (0,)), ((), ()))


def _mm(a, b, dims=_NN):
    return lax.dot_general(a.astype(BF16), b.astype(BF16), dims, preferred_element_type=F32)


def _sel_l(m, x, dims=_NN):
    out = None
    for p in _split3(x):
        t = lax.dot_general(m, p, dims, preferred_element_type=F32)
        out = t if out is None else out + t
    return out


def _sel_r(x, m, dims=_NN):
    out = None
    for p in _split3(x):
        t = lax.dot_general(p, m, dims, preferred_element_type=F32)
        out = t if out is None else out + t
    return out


def _mm3(a, b):
    ah, al = _split2(a)
    bh, bl = _split2(b)
    d = functools.partial(jnp.dot, preferred_element_type=F32)
    return d(ah, bh) + d(ah, bl) + d(al, bh)


def _sigmoid(x):
    return jax.nn.sigmoid(x)


def _silu(x):
    return x * jax.nn.sigmoid(x)


def _rms_rows(x, g):
    return x * lax.rsqrt(jnp.mean(x * x, axis=-1, keepdims=True) + EPS) * g


def _group_rms(x, gain, bd, gsz):
    w = x.shape[1]
    x2 = x * x
    parts = []
    for c in range(0, w, 256):
        parts.append(_sel_r(x2[:, c:c + 256], bd))
    ms = (parts[0] if len(parts) == 1 else jnp.concatenate(parts, axis=1)) * (1.0 / gsz)
    return x * lax.rsqrt(ms + EPS) * gain


def _tri(n, kind):
    r, c = _iota((n, n), 0), _iota((n, n), 1)
    if kind == "incl":
        return r >= c
    if kind == "strict":
        return r > c
    if kind == "upper_incl":
        return r <= c
    if kind == "upper_strict":
        return r < c
    raise ValueError(kind)


def _b01(mask):
    return jnp.where(mask, 1.0, 0.0).astype(BF16)


def _proj_body(x_ref, g_ref, w_ref, o_ref, *, tn):
    xn = _rms_rows(x_ref[...], g_ref[...]).astype(BF16)
    wp = w_ref.shape[1]
    for c in range(0, wp, tn):
        e = min(c + tn, wp)
        o_ref[:, c:e] = jnp.dot(xn, w_ref[:, c:e], preferred_element_type=F32)


def _norm_proj(h, g, w, *, tm, tn=512):
    n, d = h.shape
    wp = w.shape[1]
    return pl.pallas_call(
        functools.partial(_proj_body, tn=tn),
        out_shape=jax.ShapeDtypeStruct((n, wp), F32),
        grid=(n // tm,),
        in_specs=[pl.BlockSpec((tm, d), lambda i: (i, 0)),
                  pl.BlockSpec((1, d), lambda i: (0, 0)),
                  pl.BlockSpec((d, wp), lambda i: (0, 0), pipeline_mode=pl.Buffered(1))],
        out_specs=pl.BlockSpec((tm, wp), lambda i: (i, 0)),
        compiler_params=_cparams(("parallel",)),
        name="norm_proj",
    )(h, g, w)


def _out_mlp_body(h_ref, mo_ref, xo_ref, wo1_ref, wo2_ref, g_ref, wu_ref, wd_ref, o_ref, *, tf):
    h2 = (h_ref[...]
          + jnp.dot(mo_ref[...].astype(BF16), wo1_ref[...], preferred_element_type=F32)
          + jnp.dot(xo_ref[...].astype(BF16), wo2_ref[...], preferred_element_type=F32))
    xn = _rms_rows(h2, g_ref[...]).astype(BF16)
    acc = None
    dff = wu_ref.shape[1]
    for c in range(0, dff, tf):
        up = jnp.dot(xn, wu_ref[:, c:c + tf], preferred_element_type=F32)
        act = jnp.square(jnp.maximum(up, 0.0)).astype(BF16)
        t = jnp.dot(act, wd_ref[c:c + tf, :], preferred_element_type=F32)
        acc = t if acc is None else acc + t
    o_ref[...] = h2 + acc


def _out_mlp(h, mo, xo, wo1, wo2, g, wu, wd, *, tm, tf=512):
    n, d = h.shape
    dff = wu.shape[1]
    xw = xo.shape[1]
    const = lambda i: (0, 0)
    one = pl.Buffered(1)
    return pl.pallas_call(
        functools.partial(_out_mlp_body, tf=tf),
        out_shape=jax.ShapeDtypeStruct((n, d), F32),
        grid=(n // tm,),
        in_specs=[pl.BlockSpec((tm, d), lambda i: (i, 0)),
                  pl.BlockSpec((tm, d), lambda i: (i, 0)),
                  pl.BlockSpec((tm, xw), lambda i: (i, 0)),
                  pl.BlockSpec((d, d), const, pipeline_mode=one),
                  pl.BlockSpec((xw, d), const, pipeline_mode=one),
                  pl.BlockSpec((1, d), const),
                  pl.BlockSpec((d, dff), const, pipeline_mode=one),
                  pl.BlockSpec((dff, d), const, pipeline_mode=one)],
        out_specs=pl.BlockSpec((tm, d), lambda i: (i, 0)),
        compiler_params=_cparams(("parallel",)),
        name="out_mlp",
    )(h, mo, xo, wo1, wo2, g, wu, wd)


def _memkv_body(x_ref, g_ref, wt_ref, kn_ref, k_ref, v_ref):
    xn = _rms_rows(x_ref[...], g_ref[...]).astype(BF16)
    kvt = lax.dot_general(wt_ref[...], xn, _NT, preferred_element_type=F32)
    kn = kn_ref[...]
    for hd in range(XA_HEADS):
        kt = kvt[hd * XA_DIM:(hd + 1) * XA_DIM, :]
        ms = jnp.mean(kt * kt, axis=0, keepdims=True)
        k_ref[hd * XA_DIM:(hd + 1) * XA_DIM, :] = kt * lax.rsqrt(ms + EPS) * kn
    v_ref[...] = kvt[XA_WIDTH:, :]


def _memory_kv(mem, mem_norm, w_kv_t, knorm_col):
    b, nm, d = mem.shape
    nl = w_kv_t.shape[0]
    out = jax.ShapeDtypeStruct((nl, b, XA_WIDTH, nm), F32)
    return pl.pallas_call(
        _memkv_body,
        out_shape=(out, out),
        grid=(nl, b),
        in_specs=[pl.BlockSpec((None, nm, d), lambda l, i: (i, 0, 0)),
                  pl.BlockSpec((None, 1, d), lambda l, i: (l, 0, 0)),
                  pl.BlockSpec((None, 2 * XA_WIDTH, d), lambda l, i: (l, 0, 0)),
                  pl.BlockSpec((None, XA_DIM, 1), lambda l, i: (l, 0, 0))],
        out_specs=(pl.BlockSpec((None, None, XA_WIDTH, nm), lambda l, i: (l, i, 0, 0)),
                   pl.BlockSpec((None, None, XA_WIDTH, nm), lambda l, i: (l, i, 0, 0))),
        compiler_params=_cparams(("parallel", "parallel")),
        name="memory_kv",
    )(mem, mem_norm, w_kv_t, knorm_col)


def _xattn_body(q_ref, mk_ref, mv_ref, qn_ref, bd_ref, o_ref):
    q = _group_rms(q_ref[...], qn_ref[...], bd_ref[...], XA_DIM) * (XA_DIM ** -0.5)
    mkt = mk_ref[...].astype(BF16)
    mvt = mv_ref[...]
    lane = _iota((1, XA_WIDTH), 1) // XA_DIM
    row = _iota((XA_WIDTH, 1), 0) // XA_DIM
    out = None
    for hd in range(XA_HEADS):
        s = jnp.dot(jnp.where(lane == hd, q, 0.0).astype(BF16), mkt, preferred_element_type=F32)
        s = s - jnp.max(s, axis=-1, keepdims=True)
        p = jnp.exp(s)
        p = p / jnp.sum(p, axis=-1, keepdims=True)
        t = lax.dot_general(p.astype(BF16), jnp.where(row == hd, mvt, 0.0).astype(BF16), _NT, preferred_element_type=F32)
        out = t if out is None else out + t
    o_ref[...] = out


def _mem_attend(pp, mk, mv, qn_t, bd64, *, groups, t, tt, row_off, col_blk):
    nt = t // tt
    off = row_off // tt
    return pl.pallas_call(
        _xattn_body,
        out_shape=jax.ShapeDtypeStruct((groups * t, XA_WIDTH), F32),
        grid=(groups, nt),
        in_specs=[pl.BlockSpec((tt, XA_WIDTH), lambda g, i: (off + g * nt + i, col_blk)),
                  pl.BlockSpec((None, N_MEM, XA_WIDTH), lambda g, i: (g, 0, 0)),
                  pl.BlockSpec((None, N_MEM, XA_WIDTH), lambda g, i: (g, 0, 0)),
                  pl.BlockSpec((1, XA_WIDTH), lambda g, i: (0, 0)),
                  pl.BlockSpec((256, 256), lambda g, i: (0, 0))],
        out_specs=pl.BlockSpec((tt, XA_WIDTH), lambda g, i: (g * nt + i, 0)),
        compiler_params=_cparams(("parallel", "parallel")),
        name="mem_attend",
    )(pp, mk, mv, qn_t, bd64)


def _gdn_body(qkv_ref, z_ref, ba_ref, conv0_ref, s0_ref, cw_ref, hp_ref, nw_ref,
              o_ref, sout_ref, xbuf, cbuf, s_scr, *, c):
    j = pl.program_id(1)
    kw = N_HEADS128 * HEAD128

    @pl.when(j == 0)
    def _():
        xbuf[0:8, :] = conv0_ref[...]
        s_scr[...] = s0_ref[...]

    x = qkv_ref[...]
    xbuf[8:8 + c, :] = x
    cw = cw_ref[...]
    conv = x * cw[3:4, :]
    for i in range(1, 4):
        conv = conv + xbuf[8 - i:8 - i + c, :] * cw[3 - i:4 - i, :]
    cbuf[...] = _silu(conv)
    xbuf[0:8, :] = xbuf[c:c + 8, :]

    ba = ba_ref[...]
    hp = hp_ref[...]
    beta_all = _sigmoid(ba)
    g_all = -jnp.exp(hp[0:1, :]) * jax.nn.softplus(ba + hp[1:2, :])
    incl = _tri(c, "incl")
    strict = _tri(c, "strict")
    gcum = _sel_l(_b01(incl), g_all)
    gcum_t = _sel_r(g_all, _b01(_tri(c, "upper_incl")), _TN)
    eg_all = jnp.exp(gcum)
    eye = jnp.where(_iota((c, c), 0) == _iota((c, c), 1), 1.0, 0.0)
    nw = nw_ref[...]

    for hd in range(N_HEADS128):
        lo = hd * HEAD128
        q = cbuf[:, lo:lo + HEAD128]
        k = cbuf[:, kw + lo:kw + lo + HEAD128]
        v = cbuf[:, 2 * kw + lo:2 * kw + lo + HEAD128]
        q = q * lax.rsqrt(jnp.sum(q * q, axis=-1, keepdims=True) + EPS) * (HEAD128 ** -0.5)
        k = k * lax.rsqrt(jnp.sum(k * k, axis=-1, keepdims=True) + EPS)
        bcol = beta_all[:, hd:hd + 1]
        gc = gcum[:, 8 + hd:9 + hd]
        gr = gcum_t[8 + hd:9 + hd, :]
        egc = eg_all[:, 8 + hd:9 + hd]
        gl = gcum[c - 1:c, 8 + hd:9 + hd]
        decay = jnp.exp(jnp.where(incl, gc - gr, -jnp.inf))
        a = jnp.where(strict, bcol * _mm(k, k, _NT) * decay, 0.0)
        inv = eye - a
        pw = _mm3(a, a)
        inv = inv + _mm3(inv, pw)
        n = 2
        while 2 * n < c:
            pw = _mm3(pw, pw)
            inv = inv + _mm3(inv, pw)
            n *= 2
        s = s_scr[hd]
        sol_v = _mm3(inv, v * bcol)
        sol_k = _mm3(inv, k * (bcol * egc))
        u = sol_v - _mm(sol_k, s)
        att = _mm(q, k, _NT) * decay
        o = _mm(q * egc, s) + _mm(att, u)
        s_scr[hd] = s * jnp.exp(gl) + _mm(k * jnp.exp(gl - gc), u, _TN)
        zz = z_ref[:, lo:lo + HEAD128]
        o_ref[:, lo:lo + HEAD128] = _rms_rows(o, nw) * _silu(zz)

    @pl.when(j == pl.num_programs(1) - 1)
    def _():
        sout_ref[...] = s_scr[...]


def _gdn(pp, conv0, s0, cw, hp, nw, *, groups, t, c, row_off):
    nt = t // c
    off = row_off // c
    cc = 3 * N_HEADS128 * HEAD128
    vw = N_HEADS128 * HEAD128
    return pl.pallas_call(
        functools.partial(_gdn_body, c=c),
        out_shape=(jax.ShapeDtypeStruct((groups * t, vw), F32),
                   jax.ShapeDtypeStruct((groups, N_HEADS128, HEAD128, HEAD128), F32)),
        grid=(groups, nt),
        in_specs=[pl.BlockSpec((c, cc), lambda g, i: (off + g * nt + i, 0)),
                  pl.BlockSpec((c, vw), lambda g, i: (off + g * nt + i, cc // vw)),
                  pl.BlockSpec((c, LANES), lambda g, i: (off + g * nt + i, (cc + vw + XA_WIDTH) // LANES)),
                  pl.BlockSpec((None, 8, cc), lambda g, i: (g, 0, 0)),
                  pl.BlockSpec((None, N_HEADS128, HEAD128, HEAD128), lambda g, i: (g, 0, 0, 0)),
                  pl.BlockSpec((4, cc), lambda g, i: (0, 0)),
                  pl.BlockSpec((2, LANES), lambda g, i: (0, 0)),
                  pl.BlockSpec((1, HEAD128), lambda g, i: (0, 0))],
        out_specs=(pl.BlockSpec((c, vw), lambda g, i: (g * nt + i, 0)),
                   pl.BlockSpec((None, N_HEADS128, HEAD128, HEAD128), lambda g, i: (g, 0, 0, 0))),
        scratch_shapes=[pltpu.VMEM((c + 8, cc), F32), pltpu.VMEM((c, cc), F32),
                        pltpu.VMEM((N_HEADS128, HEAD128, HEAD128), F32)],
        compiler_params=_cparams(("parallel", "arbitrary")),
        name="gdn",
    )(pp, pp, pp, conv0, s0, cw, hp, nw)


def _hgrn_body(q_ref, f_ref, i_ref, g_ref, s0_ref, lb_ref, nw_ref, o_ref, sout_ref, st_scr, *, r, c):
    j = pl.program_id(1)

    @pl.when(j == 0)
    def _():
        for hd in range(N_HEADS128):
            st_scr[hd] = s0_ref[hd].T

    nw = nw_ref[...]
    cum_m = _b01(_tri(c, "incl"))
    rows = _iota((c, 1), 0)
    nsub = c // HG_SUB
    for hd in range(N_HEADS128):
        lo = hd * HEAD128
        lb = lb_ref[:, lo:lo + HEAD128]
        log_lb = jnp.log(lb)
        log_1m = jnp.log1p(-lb)
        for ch in range(r // c):
            r0 = ch * c
            ff = f_ref[r0:r0 + c, lo:lo + HEAD128]
            logf = jnp.logaddexp(log_lb, log_1m + jax.nn.log_sigmoid(ff))
            k = (1.0 - lb) * _sigmoid(-ff)
            q = _silu(q_ref[r0:r0 + c, lo:lo + HEAD128]) * (HEAD128 ** -0.5)
            v = i_ref[r0:r0 + c, lo:lo + HEAD128]
            bc = _sel_l(cum_m, logf)
            bl = bc[c - 1:c, :]
            st = st_scr[hd]
            o = _mm(q * jnp.exp(bc), st, _NT)
            if nsub > 1:
                att = None
                for sb in range(1, nsub):
                    bref = bc[sb * HG_SUB - 1:sb * HG_SUB, :]
                    in_blk = (rows >= sb * HG_SUB) & (rows < (sb + 1) * HG_SUB)
                    qs = jnp.where(in_blk, q * jnp.exp(jnp.minimum(bc - bref, 0.0)), 0.0)
                    ks = jnp.where(rows < sb * HG_SUB, k * jnp.exp(jnp.minimum(bref - bc, 0.0)), 0.0)
                    t_ = _mm(qs, ks, _NT)
                    att = t_ if att is None else att + t_
                o = o + _mm(att, v)
            diag = []
            tri8 = _iota((HG_SUB, 1), 0)
            for sb in range(nsub):
                b0 = sb * HG_SUB
                qb, bcb = q[b0:b0 + HG_SUB, :], bc[b0:b0 + HG_SUB, :]
                ob = None
                for s_ in range(HG_SUB):
                    w = jnp.exp(jnp.minimum(bcb - bc[b0 + s_:b0 + s_ + 1, :], 0.0))
                    col = jnp.sum(qb * k[b0 + s_:b0 + s_ + 1, :] * w, axis=-1, keepdims=True)
                    col = jnp.where(tri8 >= s_, col, 0.0)
                    t_ = col * v[b0 + s_:b0 + s_ + 1, :]
                    ob = t_ if ob is None else ob + t_
                diag.append(ob)
            o = o + (diag[0] if nsub == 1 else jnp.concatenate(diag, axis=0))
            st_scr[hd] = st * jnp.exp(bl) + _mm(v, k * jnp.exp(bl - bc), _TN)
            gg = g_ref[r0:r0 + c, lo:lo + HEAD128]
            o_ref[r0:r0 + c, lo:lo + HEAD128] = _rms_rows(o, nw) * _silu(gg)

    @pl.when(j == pl.num_programs(1) - 1)
    def _():
        for hd in range(N_HEADS128):
            sout_ref[hd] = st_scr[hd].T


def _hgrn(pp, s0, lb, nw, *, groups, t, r, c, row_off):
    nt = t // r
    off = row_off // r
    w = N_HEADS128 * HEAD128
    rows = lambda col: pl.BlockSpec((r, w), lambda g, i: (off + g * nt + i, col))
    return pl.pallas_call(
        functools.partial(_hgrn_body, r=r, c=c),
        out_shape=(jax.ShapeDtypeStruct((groups * t, w), F32),
                   jax.ShapeDtypeStruct((groups, N_HEADS128, HEAD128, HEAD128), F32)),
        grid=(groups, nt),
        in_specs=[rows(0), rows(1), rows(2), rows(3),
                  pl.BlockSpec((None, N_HEADS128, HEAD128, HEAD128), lambda g, i: (g, 0, 0, 0)),
                  pl.BlockSpec((1, w), lambda g, i: (0, 0)),
                  pl.BlockSpec((1, HEAD128), lambda g, i: (0, 0))],
        out_specs=(pl.BlockSpec((r, w), lambda g, i: (g * nt + i, 0)),
                   pl.BlockSpec((None, N_HEADS128, HEAD128, HEAD128), lambda g, i: (g, 0, 0, 0))),
        scratch_shapes=[pltpu.VMEM((N_HEADS128, HEAD128, HEAD128), F32)],
        compiler_params=_cparams(("parallel", "arbitrary")),
        name="hgrn2",
    )(pp, pp, pp, pp, s0, lb, nw)


def _fox_prep_body(q_ref, k_ref, fl_ref, qn_ref, kn_ref, fb_ref, bd_ref, qo_ref, ko_ref, lf_ref, fc_ref, carry):
    j = pl.program_id(1)

    @pl.when(j == 0)
    def _():
        carry[...] = jnp.zeros_like(carry)

    bd = bd_ref[...]
    qo_ref[...] = _group_rms(q_ref[...], qn_ref[...], bd, FOX_DIM)
    ko_ref[...] = _group_rms(k_ref[...], kn_ref[...], bd, FOX_DIM)
    lf = jax.nn.log_sigmoid(fl_ref[...] + fb_ref[...])
    lf_ref[...] = lf
    tr = lf.shape[0]
    fc = _sel_l(_b01(_tri(tr, "incl")), lf) + carry[...]
    fc_ref[...] = fc
    carry[...] = fc[tr - 1:tr, :]


def _fox_prep(pp, qn_t, kn_t, fb, bd64, *, groups, t, tr, row_off):
    nt = t // tr
    off = row_off // tr
    w = FOX_HEADS * FOX_DIM
    n = groups * t
    big = jax.ShapeDtypeStruct((n, w), F32)
    small = jax.ShapeDtypeStruct((n, LANES), F32)
    return pl.pallas_call(
        _fox_prep_body,
        out_shape=(big, big, small, small),
        grid=(groups, nt),
        in_specs=[pl.BlockSpec((tr, w), lambda g, i: (off + g * nt + i, 0)),
                  pl.BlockSpec((tr, w), lambda g, i: (off + g * nt + i, 1)),
                  pl.BlockSpec((tr, LANES), lambda g, i: (off + g * nt + i, (4 * w + XA_WIDTH) // LANES)),
                  pl.BlockSpec((1, w), lambda g, i: (0, 0)),
                  pl.BlockSpec((1, w), lambda g, i: (0, 0)),
                  pl.BlockSpec((1, LANES), lambda g, i: (0, 0)),
                  pl.BlockSpec((256, 256), lambda g, i: (0, 0))],
        out_specs=(pl.BlockSpec((tr, w), lambda g, i: (g * nt + i, 0)),
                   pl.BlockSpec((tr, w), lambda g, i: (g * nt + i, 0)),
                   pl.BlockSpec((tr, LANES), lambda g, i: (g * nt + i, 0)),
                   pl.BlockSpec((tr, LANES), lambda g, i: (g * nt + i, 0))),
        scratch_shapes=[pltpu.VMEM((1, LANES), F32)],
        compiler_params=_cparams(("parallel", "arbitrary")),
        name="fox_prep",
    )(pp, pp, pp, qn_t, kn_t, fb, bd64)


def _fox_flash_body(q_ref, k_ref, v_ref, og_ref, fq_ref, fk_ref, o_ref, m_scr, l_scr, acc_scr, *, tq):
    hp = pl.program_id(1)
    qi = pl.program_id(2)
    lane = _iota((1, LANES), 1)
    q = q_ref[...] * (FOX_DIM ** -0.5)
    fq = fq_ref[...]
    qs, fqs = [], []
    for j in range(2):
        qs.append(jnp.where((lane // FOX_DIM) == j, q, 0.0).astype(BF16))
        fqs.append(jnp.sum(jnp.where(lane == 2 * hp + j, fq, 0.0), axis=-1, keepdims=True))
        m_scr[j] = jnp.full((tq, 1), -jnp.inf, F32)
        l_scr[j] = jnp.zeros((tq, 1), F32)
        acc_scr[j] = jnp.zeros((tq, LANES), F32)

    def block(ki, masked):
        k0 = pl.multiple_of(ki * tq, tq)
        kb = k_ref[pl.ds(k0, tq), :].astype(BF16)
        vb = v_ref[pl.ds(k0, tq), :].astype(BF16)
        for j in range(2):
            fk = fk_ref[pl.ds(2 * hp + j, 1), pl.ds(k0, tq)]
            s = lax.dot_general(qs[j], kb, _NT, preferred_element_type=F32) + fqs[j] - fk
            if masked:
                s = jnp.where(_tri(tq, "incl"), s, -jnp.inf)
            m_old = m_scr[j]
            m_new = jnp.maximum(m_old, jnp.max(s, axis=-1, keepdims=True))
            alpha = jnp.exp(m_old - m_new)
            p = jnp.exp(s - m_new)
            l_scr[j] = alpha * l_scr[j] + jnp.sum(p, axis=-1, keepdims=True)
            acc_scr[j] = alpha * acc_scr[j] + jnp.dot(p.astype(BF16), vb, preferred_element_type=F32)
            m_scr[j] = m_new

    def body(ki, carry):
        block(ki, False)
        return carry

    lax.fori_loop(0, qi, body, 0)
    block(qi, True)
    o0 = acc_scr[0] / l_scr[0]
    o1 = acc_scr[1] / l_scr[1]
    o_ref[...] = jnp.where(lane < FOX_DIM, o0, o1) * _sigmoid(og_ref[...])


def _fox_flash(qn, kn, pp, fc, fct, *, groups, t, tq):
    nq = t // tq
    w = FOX_HEADS * FOX_DIM
    hp_n = w // LANES
    vblk = 2 * w // LANES
    gblk = 3 * w // LANES
    return pl.pallas_call(
        functools.partial(_fox_flash_body, tq=tq),
        out_shape=jax.ShapeDtypeStruct((groups * t, w), F32),
        grid=(groups, hp_n, nq),
        in_specs=[pl.BlockSpec((tq, LANES), lambda g, h, i: (g * nq + i, h)),
                  pl.BlockSpec((t, LANES), lambda g, h, i: (g, h)),
                  pl.BlockSpec((t, LANES), lambda g, h, i: (g, vblk + h)),
                  pl.BlockSpec((tq, LANES), lambda g, h, i: (g * nq + i, gblk + h)),
                  pl.BlockSpec((tq, LANES), lambda g, h, i: (g * nq + i, 0)),
                  pl.BlockSpec((None, FOX_HEADS, t), lambda g, h, i: (g, 0, 0))],
        out_specs=pl.BlockSpec((tq, LANES), lambda g, h, i: (g * nq + i, h)),
        scratch_shapes=[pltpu.VMEM((2, tq, 1), F32), pltpu.VMEM((2, tq, 1), F32), pltpu.VMEM((2, tq, LANES), F32)],
        compiler_params=_cparams(("parallel", "parallel", "arbitrary")),
        name="fox_flash",
    )(qn, kn, pp, pp, fc, fct)


def _fox_sample_body(pt_ref, *refs, gp, tnew):
    del pt_ref
    k_refs = refs[0:gp]
    v_refs = refs[gp:2 * gp]
    lf_refs = refs[2 * gp:3 * gp]
    q_ref, kn_ref, vn_ref, og_ref, lfn_ref, o_ref, q_scr, m_scr, l_scr, acc, carry = refs[3 * gp:]
    pg = pl.program_id(1)

    @pl.when(pg == 0)
    def _():
        qs = q_ref[...] * (FOX_DIM ** -0.5)
        for hd in range(FOX_HEADS):
            q_scr[hd] = qs[:, hd * FOX_DIM:(hd + 1) * FOX_DIM]
        m_scr[...] = jnp.full(m_scr.shape, -jnp.inf, F32)
        l_scr[...] = jnp.zeros_like(l_scr)
        acc[...] = jnp.zeros_like(acc)
        carry[...] = jnp.zeros_like(carry)

    fnew = _sel_l(_b01(_tri(tnew, "incl")), lfn_ref[...])

    def update(hd, s, pv_fn):
        m_old = m_scr[hd]
        m_new = jnp.maximum(m_old, jnp.max(s, axis=-1, keepdims=True))
        alpha = jnp.exp(m_old - m_new)
        p = jnp.exp(s - m_new)
        l_scr[hd] = alpha * l_scr[hd] + jnp.sum(p, axis=-1, keepdims=True)
        acc[hd] = alpha * acc[hd] + pv_fn(p.astype(BF16))
        m_scr[hd] = m_new

    later = _b01(_tri(PAGE, "strict"))
    rests = []
    run = carry[...]
    for j in range(gp):
        lft = lf_refs[j][...]
        rests.append(_sel_r(lft, later) + run)
        run = run + jnp.sum(lft, axis=-1, keepdims=True)
    carry[...] = run

    for hd in range(FOX_HEADS):
        qh = q_scr[hd].astype(BF16)
        fn_h = fnew[:, hd:hd + 1]
        vts = [v_refs[j][hd].astype(BF16) for j in range(gp)]
        s = jnp.concatenate(
            [jnp.dot(qh, k_refs[j][hd].astype(BF16), preferred_element_type=F32) + rests[j][hd:hd + 1, :]
             for j in range(gp)], axis=1) + fn_h

        def pv(p, vts=vts):
            out = None
            for j in range(gp):
                t_ = lax.dot_general(p[:, j * PAGE:(j + 1) * PAGE], vts[j], _NT, preferred_element_type=F32)
                out = t_ if out is None else out + t_
            return out

        update(hd, s, pv)

    @pl.when(pg == pl.num_programs(1) - 1)
    def _():
        fnew_t = fnew.T
        causal = _tri(tnew, "incl")
        outs = []
        for hd in range(FOX_HEADS):
            lo = hd * FOX_DIM
            qh = q_scr[hd].astype(BF16)
            s = lax.dot_general(qh, kn_ref[:, lo:lo + FOX_DIM].astype(BF16), _NT, preferred_element_type=F32)
            s = s + fnew[:, hd:hd + 1] - fnew_t[hd:hd + 1, :]
            vn = vn_ref[:, lo:lo + FOX_DIM].astype(BF16)
            update(hd, jnp.where(causal, s, -jnp.inf), lambda p, vn=vn: jnp.dot(p, vn, preferred_element_type=F32))
            outs.append(acc[hd] / l_scr[hd])
        o_ref[...] = jnp.concatenate(outs, axis=1) * _sigmoid(og_ref[...])


def _fox_sample(page_table, ckt, cvt, clft, qn, kn, pp, lfn, *, tnew, row_off, gp):
    ns, npages = page_table.shape
    w = FOX_HEADS * FOX_DIM
    ngrp = npages // gp
    off = row_off // tnew

    def page_map(j, nd):
        return lambda s, g, pt: (pt[s, npages - 1 - (g * gp + j)],) + (0,) * nd

    kv_specs = [pl.BlockSpec((None, FOX_HEADS, FOX_DIM, PAGE), page_map(j, 3)) for j in range(gp)]
    lf_specs = [pl.BlockSpec((None, FOX_HEADS, PAGE), page_map(j, 2)) for j in range(gp)]
    row = lambda s, g, pt: (s, 0)
    grid_spec = pltpu.PrefetchScalarGridSpec(
        num_scalar_prefetch=1,
        grid=(ns, ngrp),
        in_specs=kv_specs + kv_specs + lf_specs + [
            pl.BlockSpec((tnew, w), row),
            pl.BlockSpec((tnew, w), row),
            pl.BlockSpec((tnew, w), lambda s, g, pt: (off + s, 2)),
            pl.BlockSpec((tnew, w), lambda s, g, pt: (off + s, 3)),
            pl.BlockSpec((tnew, LANES), row)],
        out_specs=pl.BlockSpec((tnew, w), row),
        scratch_shapes=[pltpu.VMEM((FOX_HEADS, tnew, FOX_DIM), F32), pltpu.VMEM((FOX_HEADS, tnew, 1), F32),
                        pltpu.VMEM((FOX_HEADS, tnew, 1), F32), pltpu.VMEM((FOX_HEADS, tnew, FOX_DIM), F32),
                        pltpu.VMEM((FOX_HEADS, 1), F32)],
    )
    return pl.pallas_call(
        functools.partial(_fox_sample_body, gp=gp, tnew=tnew),
        out_shape=jax.ShapeDtypeStruct((ns * tnew, w), F32),
        grid_spec=grid_spec,
        compiler_params=_cparams(("parallel", "arbitrary")),
        name="fox_sample",
    )(page_table, *([ckt] * gp), *([cvt] * gp), *([clft] * gp), qn, kn, pp, pp, lfn)


def _cmlp_body(u_ref, v_ref, lg_ref, lbias_ref, ws_ref, bsb_ref, o_ref, vo_ref):
    u = jax.nn.gelu(u_ref[...], approximate=True)
    z = jax.nn.gelu(v_ref[...], approximate=True)
    mu = jnp.mean(z, axis=-1, keepdims=True)
    zc = z - mu
    v = zc * lax.rsqrt(jnp.mean(zc * zc, axis=-1, keepdims=True) + EPS) * lg_ref[...] + lbias_ref[...]
    vo_ref[...] = v
    tril = _tri(CM_CHUNK, "incl")
    gd = v.shape[1] // CM_GROUPS
    for g in range(CM_GROUPS):
        wm = jnp.where(tril, ws_ref[g], 0.0)
        mixed = _mm(wm, v[:, g * gd:(g + 1) * gd]) + bsb_ref[g]
        o_ref[:, g * gd:(g + 1) * gd] = u[:, g * gd:(g + 1) * gd] * mixed


def _chunk_mlp(pp, ln_g, ln_b, ws, bsb, *, nrows, row_off):
    w = ln_g.shape[1]
    off = row_off // CM_CHUNK
    out = jax.ShapeDtypeStruct((nrows, w), F32)
    const2 = lambda i: (0, 0)
    const3 = lambda i: (0, 0, 0)
    return pl.pallas_call(
        _cmlp_body,
        out_shape=(out, out),
        grid=(nrows // CM_CHUNK,),
        in_specs=[pl.BlockSpec((CM_CHUNK, w), lambda i: (off + i, 0)),
                  pl.BlockSpec((CM_CHUNK, w), lambda i: (off + i, 1)),
                  pl.BlockSpec((1, w), const2), pl.BlockSpec((1, w), const2),
                  pl.BlockSpec((CM_GROUPS, CM_CHUNK, CM_CHUNK), const3),
                  pl.BlockSpec((CM_GROUPS, CM_CHUNK, w // CM_GROUPS), const3)],
        out_specs=(pl.BlockSpec((CM_CHUNK, w), lambda i: (i, 0)),
                   pl.BlockSpec((CM_CHUNK, w), lambda i: (i, 0))),
        compiler_params=_cparams(("parallel",)),
        name="chunk_mlp",
    )(pp, pp, ln_g, ln_b, ws, bsb)


def _pad_cols(w, mult):
    pad = (-w.shape[1]) % mult
    return jnp.pad(w, ((0, 0), (0, pad))) if pad else w


def _lane_row(v, start=0):
    return jnp.zeros((1, LANES), F32).at[0, start:start + v.shape[0]].set(v.astype(F32))


def kernel(x_prompt, x_sample, mem_prompt, state_a_conv, state_a_ssm, state_b_ssm, cache_c_k, cache_c_v, cache_c_logf, cache_mem_k, cache_mem_v, page_table, norm_mix, w_out, norm_mlp, w_up, w_down, mem_norm, w_mem_kv, xa_qnorm, xa_knorm, w_in_a, a_conv_w, a_log, a_dt_bias, a_norm_w, w_in_b, hg_lb, b_norm_w, w_in_c, c_fbias, c_qnorm, c_knorm, w_in_d, d_ln_g, d_ln_b, d_ws, d_bs):
    bp, seq, d = x_prompt.shape
    ds, dseq, _ = x_sample.shape
    depth = norm_mix.shape[0]
    np_rows, ns_rows = bp * seq, ds * dseq
    n = np_rows + ns_rows
    tm = 640 if n % 640 == 0 else 128
    w1024 = N_HEADS128 * HEAD128
    cc = 3 * w1024

    h = jnp.concatenate([x_prompt.reshape(np_rows, d), x_sample.reshape(ns_rows, d)], axis=0)
    bd64 = (jnp.arange(256)[:, None] // 64 == jnp.arange(256)[None, :] // 64).astype(BF16)

    mk_all, mv_all = _memory_kv(mem_prompt, mem_norm[:, None, :], jnp.swapaxes(w_mem_kv, 1, 2).astype(BF16),
                                xa_knorm[:, :, None])

    lb_w = jax.nn.softmax(hg_lb.astype(F32), axis=0)
    lower_bounds = jnp.cumsum(lb_w, axis=0) - lb_w[0]

    outs = {}
    for l in range(depth):
        kind, j = l % 4, l // 4
        if kind == 0:
            wi = w_in_a[j]
            w_packed = jnp.concatenate([wi[:, :cc + w1024], wi[:, cc + w1024 + 16:], _pad_cols(wi[:, cc + w1024:cc + w1024 + 16], LANES)], axis=1)
            xq_blk = (cc + w1024) // XA_WIDTH
        elif kind == 1:
            w_packed = w_in_b[j]
            xq_blk = 4 * w1024 // XA_WIDTH
        elif kind == 2:
            wi = w_in_c[j]
            w_packed = jnp.concatenate([wi[:, :4 * w1024], wi[:, 4 * w1024 + 16:], _pad_cols(wi[:, 4 * w1024:4 * w1024 + 16], LANES)], axis=1)
            xq_blk = 4 * w1024 // XA_WIDTH
        else:
            w_packed = w_in_d[j]
            xq_blk = 2 * w1024 // XA_WIDTH
        pp = _norm_proj(h, norm_mix[l][None, :], w_packed.astype(BF16), tm=tm)

        qn_t = jnp.tile(xa_qnorm[l], XA_HEADS)[None, :]
        xo_p = _mem_attend(pp, mk_all[l], mv_all[l], qn_t, bd64, groups=bp, t=seq, tt=512, row_off=0, col_blk=xq_blk)
        to_hdn = lambda a: jnp.transpose(a, (0, 2, 3, 1)).reshape(ds, XA_WIDTH, N_MEM)
        xo_s = _mem_attend(pp, to_hdn(cache_mem_k[l]), to_hdn(cache_mem_v[l]),
                           qn_t, bd64, groups=ds, t=dseq, tt=dseq, row_off=np_rows, col_blk=xq_blk)

        if kind == 0:
            hp = jnp.concatenate([_lane_row(a_log[j], 8), _lane_row(a_dt_bias[j], 8)], axis=0)
            nw = a_norm_w[j][None, :]
            conv0_p = jnp.zeros((bp, 8, cc), F32)
            conv0_s = jnp.pad(state_a_conv[j], ((0, 0), (5, 0), (0, 0)))
            mo_p, st_p = _gdn(pp, conv0_p, jnp.zeros((bp, N_HEADS128, HEAD128, HEAD128), F32), a_conv_w[j], hp, nw,
                              groups=bp, t=seq, c=GDN_CHUNK, row_off=0)
            mo_s, st_s = _gdn(pp, conv0_s, state_a_ssm[j], a_conv_w[j], hp, nw,
                              groups=ds, t=dseq, c=dseq, row_off=np_rows)
            outs.setdefault("a_conv_p", []).append(pp[:np_rows, :cc].reshape(bp, seq, cc)[:, seq - 3:])
            outs.setdefault("a_conv_s", []).append(pp[np_rows:, :cc].reshape(ds, dseq, cc)[:, dseq - 3:])
            outs.setdefault("a_ssm_p", []).append(st_p)
            outs.setdefault("a_ssm_s", []).append(st_s)
        elif kind == 1:
            lb = lower_bounds[l][None, :]
            nw = b_norm_w[j][None, :]
            mo_p, st_p = _hgrn(pp, jnp.zeros((bp, N_HEADS128, HEAD128, HEAD128), F32), lb, nw,
                               groups=bp, t=seq, r=128, c=HG_CHUNK, row_off=0)
            mo_s, st_s = _hgrn(pp, state_b_ssm[j], lb, nw, groups=ds, t=dseq, r=dseq, c=dseq, row_off=np_rows)
            outs.setdefault("b_ssm_p", []).append(st_p)
            outs.setdefault("b_ssm_s", []).append(st_s)
        elif kind == 2:
            qn_f = jnp.tile(c_qnorm[j], FOX_HEADS)[None, :]
            kn_f = jnp.tile(c_knorm[j], FOX_HEADS)[None, :]
            fb = _lane_row(c_fbias[j])
            q_p, k_p, lf_p, fc_p = _fox_prep(pp, qn_f, kn_f, fb, bd64, groups=bp, t=seq, tr=256, row_off=0)
            q_s, k_s, lf_s, _ = _fox_prep(pp, qn_f, kn_f, fb, bd64, groups=ds, t=dseq, tr=dseq, row_off=np_rows)
            fct = jnp.swapaxes(fc_p[:, :FOX_HEADS].reshape(bp, seq, FOX_HEADS), 1, 2)
            mo_p = _fox_flash(q_p, k_p, pp, fc_p, fct, groups=bp, t=seq, tq=FOX_TQ)
            pos_minor = lambda a: jnp.transpose(a, (0, 2, 3, 1))
            mo_s = _fox_sample(page_table, pos_minor(cache_c_k[j]), pos_minor(cache_c_v[j]),
                               jnp.swapaxes(cache_c_logf[j], 1, 2), q_s, k_s, pp, lf_s,
                               tnew=dseq, row_off=np_rows, gp=PAGES_PER_STEP)
            v_all = pp[:, 2 * w1024:3 * w1024]
            outs.setdefault("c_k_p", []).append(k_p.reshape(bp, seq, FOX_HEADS, FOX_DIM))
            outs.setdefault("c_v_p", []).append(v_all[:np_rows].reshape(bp, seq, FOX_HEADS, FOX_DIM))
            outs.setdefault("c_lf_p", []).append(lf_p[:, :FOX_HEADS].reshape(bp, seq, FOX_HEADS))
            outs.setdefault("c_k_s", []).append(k_s.reshape(ds, dseq, FOX_HEADS, FOX_DIM))
            outs.setdefault("c_v_s", []).append(v_all[np_rows:].reshape(ds, dseq, FOX_HEADS, FOX_DIM))
            outs.setdefault("c_lf_s", []).append(lf_s[:, :FOX_HEADS].reshape(ds, dseq, FOX_HEADS))
        else:
            bsb = jnp.broadcast_to(d_bs[j][:, :, None], (CM_GROUPS, CM_CHUNK, w1024 // CM_GROUPS))
            lg, lbias = d_ln_g[j][None, :], d_ln_b[j][None, :]
            mo_p, _ = _chunk_mlp(pp, lg, lbias, d_ws[j], bsb, nrows=np_rows, row_off=0)
            pps = jnp.pad(pp[np_rows:, :2 * w1024].reshape(ds, dseq, 2 * w1024), ((0, 0), (0, CM_CHUNK - dseq), (0, 0)))
            mo_s, v_s = _chunk_mlp(pps.reshape(ds * CM_CHUNK, 2 * w1024), lg, lbias, d_ws[j], bsb, nrows=ds * CM_CHUNK, row_off=0)
            mo_s = mo_s.reshape(ds, CM_CHUNK, w1024)[:, :dseq].reshape(ns_rows, w1024)
            outs.setdefault("d_v_s", []).append(v_s.reshape(ds, CM_CHUNK, w1024)[:, :dseq])

        mo = jnp.concatenate([mo_p, mo_s], axis=0)
        xo = jnp.concatenate([xo_p, xo_s], axis=0)
        wo = w_out[l].astype(BF16)
        h = _out_mlp(h, mo, xo, wo[:w1024], wo[w1024:], norm_mlp[l][None, :], w_up[l].astype(BF16), w_down[l].astype(BF16), tm=tm)

    st = lambda name: jnp.stack(outs[name])
    mem_out = lambda a: jnp.transpose(a.reshape(depth, bp, XA_HEADS, XA_DIM, N_MEM), (0, 1, 4, 2, 3))
    return (h[:np_rows].reshape(bp, seq, d), h[np_rows:].reshape(ds, dseq, d),
            st("a_conv_p"), st("a_conv_s"), st("a_ssm_p"), st("a_ssm_s"),
            st("b_ssm_p"), st("b_ssm_s"),
            st("c_k_p"), st("c_v_p"), st("c_lf_p"), st("c_k_s"), st("c_v_s"), st("c_lf_s"),
            st("d_v_s"), mem_out(mk_all), mem_out(mv_all))
```

```python
import functools
import math

import jax
import jax.numpy as jnp
from jax import lax
from jax.experimental import pallas as pl
from jax.experimental.pallas import tpu as pltpu

F32 = jnp.float32
BF16 = jnp.bfloat16
EPS = 1e-6

LANES = 128
SUBLANES = 8
VMEM_LIMIT = 56 * 1024 * 1024

XA_HEADS, XA_DIM, N_MEM = 4, 64, 256
XA_WIDTH = XA_HEADS * XA_DIM
HEAD128 = 128
N_HEADS128 = 8
FOX_DIM, FOX_HEADS = 64, 16
PAGE = 128
CM_CHUNK, CM_GROUPS = 128, 8
GDN_CHUNK, HG_CHUNK, HG_SUB = 64, 32, 8
FOX_TQ = 256
PAGES_PER_STEP = 8


def _cparams(sem):
    return pltpu.CompilerParams(dimension_semantics=sem, vmem_limit_bytes=VMEM_LIMIT)


def _iota(shape, dim):
    return lax.broadcasted_iota(jnp.int32, shape, dim)


def _split3(x):
    hi = x.astype(BF16)
    r = x - hi.astype(F32)
    mid = r.astype(BF16)
    lo = (r - mid.astype(F32)).astype(BF16)
    return hi, mid, lo


def _split2(x):
    hi = x.astype(BF16)
    lo = (x - hi.astype(F32)).astype(BF16)
    return hi, lo


_NN = (((1,), (0,)), ((), ()))
_NT = (((1,), (1,)), ((), ()))
_TN = (((0,), (0,)), ((), ()))


def _mm(a, b, dims=_NN):
    return lax.dot_general(a.astype(BF16), b.astype(BF16), dims, preferred_element_type=F32)


def _sel_l(m, x, dims=_NN):
    out = None
    for p in _split3(x):
        t = lax.dot_general(m, p, dims, preferred_element_type=F32)
        out = t if out is None else out + t
    return out


def _sel_r(x, m, dims=_NN):
    out = None
    for p in _split3(x):
        t = lax.dot_general(p, m, dims, preferred_element_type=F32)
        out = t if out is None else out + t
    return out


def _mm3(a, b):
    ah, al = _split2(a)
    bh, bl = _split2(b)
    d = functools.partial(jnp.dot, preferred_element_type=F32)
    return d(ah, bh) + d(ah, bl) + d(al, bh)


def _sigmoid(x):
    return jax.nn.sigmoid(x)


def _silu(x):
    return x * jax.nn.sigmoid(x)


def _log_sigmoid(x):
    return jnp.minimum(x, 0.0) - jnp.log1p(jnp.exp(-jnp.abs(x)))


def _logaddexp(a, b):
    return jnp.maximum(a, b) + jnp.log1p(jnp.exp(-jnp.abs(a - b)))


def _rms_rows(x, g):
    return x * lax.rsqrt(jnp.mean(x * x, axis=-1, keepdims=True) + EPS) * g


def _group_rms(x, gain, bd, gsz):
    w = x.shape[1]
    x2 = x * x
    parts = []
    for c in range(0, w, 256):
        parts.append(_sel_r(x2[:, c:c + 256], bd))
    ms = (parts[0] if len(parts) == 1 else jnp.concatenate(parts, axis=1)) * (1.0 / gsz)
    return x * lax.rsqrt(ms + EPS) * gain


def _tri(n, kind):
    r, c = _iota((n, n), 0), _iota((n, n), 1)
    if kind == "incl":
        return r >= c
    if kind == "strict":
        return r > c
    if kind == "upper_incl":
        return r <= c
    if kind == "upper_strict":
        return r < c
    raise ValueError(kind)


def _b01(mask):
    return jnp.where(mask, 1.0, 0.0).astype(BF16)


def _proj_body(x_ref, g_ref, w_ref, o_ref, *, tn):
    xn = _rms_rows(x_ref[...], g_ref[...]).astype(BF16)
    wp = w_ref.shape[1]
    for c in range(0, wp, tn):
        e = min(c + tn, wp)
        o_ref[:, c:e] = jnp.dot(xn, w_ref[:, c:e], preferred_element_type=F32)


def _norm_proj(h, g, w, *, tm, tn=512):
    n, d = h.shape
    wp = w.shape[1]
    return pl.pallas_call(
        functools.partial(_proj_body, tn=tn),
        out_shape=jax.ShapeDtypeStruct((n, wp), F32),
        grid=(n // tm,),
        in_specs=[pl.BlockSpec((tm, d), lambda i: (i, 0)),
                  pl.BlockSpec((1, d), lambda i: (0, 0)),
                  pl.BlockSpec((d, wp), lambda i: (0, 0), pipeline_mode=pl.Buffered(1))],
        out_specs=pl.BlockSpec((tm, wp), lambda i: (i, 0)),
        compiler_params=_cparams(("parallel",)),
        name="norm_proj",
    )(h, g, w)


def _out_mlp_body(h_ref, mo_ref, xo_ref, wo1_ref, wo2_ref, g_ref, wu_ref, wd_ref, o_ref, *, tf):
    h2 = (h_ref[...]
          + jnp.dot(mo_ref[...].astype(BF16), wo1_ref[...], preferred_element_type=F32)
          + jnp.dot(xo_ref[...].astype(BF16), wo2_ref[...], preferred_element_type=F32))
    xn = _rms_rows(h2, g_ref[...]).astype(BF16)
    acc = None
    dff = wu_ref.shape[1]
    for c in range(0, dff, tf):
        up = jnp.dot(xn, wu_ref[:, c:c + tf], preferred_element_type=F32)
        act = jnp.square(jnp.maximum(up, 0.0)).astype(BF16)
        t = jnp.dot(act, wd_ref[c:c + tf, :], preferred_element_type=F32)
        acc = t if acc is None else acc + t
    o_ref[...] = h2 + acc


def _out_mlp(h, mo, xo, wo1, wo2, g, wu, wd, *, tm, tf=512):
    n, d = h.shape
    dff = wu.shape[1]
    xw = xo.shape[1]
    const = lambda i: (0, 0)
    one = pl.Buffered(1)
    return pl.pallas_call(
        functools.partial(_out_mlp_body, tf=tf),
        out_shape=jax.ShapeDtypeStruct((n, d), F32),
        grid=(n // tm,),
        in_specs=[pl.BlockSpec((tm, d), lambda i: (i, 0)),
                  pl.BlockSpec((tm, d), lambda i: (i, 0)),
                  pl.BlockSpec((tm, xw), lambda i: (i, 0)),
                  pl.BlockSpec((d, d), const, pipeline_mode=one),
                  pl.BlockSpec((xw, d), const, pipeline_mode=one),
                  pl.BlockSpec((1, d), const),
                  pl.BlockSpec((d, dff), const, pipeline_mode=one),
                  pl.BlockSpec((dff, d), const, pipeline_mode=one)],
        out_specs=pl.BlockSpec((tm, d), lambda i: (i, 0)),
        compiler_params=_cparams(("parallel",)),
        name="out_mlp",
    )(h, mo, xo, wo1, wo2, g, wu, wd)


def _memkv_body(x_ref, g_ref, wt_ref, kn_ref, k_ref, v_ref):
    xn = _rms_rows(x_ref[...], g_ref[...]).astype(BF16)
    kvt = lax.dot_general(wt_ref[...], xn, _NT, preferred_element_type=F32)
    kn = kn_ref[...]
    for hd in range(XA_HEADS):
        kt = kvt[hd * XA_DIM:(hd + 1) * XA_DIM, :]
        ms = jnp.mean(kt * kt, axis=0, keepdims=True)
        k_ref[hd * XA_DIM:(hd + 1) * XA_DIM, :] = kt * lax.rsqrt(ms + EPS) * kn
    v_ref[...] = kvt[XA_WIDTH:, :]


def _memory_kv(mem, mem_norm, w_kv_t, knorm_col):
    b, nm, d = mem.shape
    nl = w_kv_t.shape[0]
    out = jax.ShapeDtypeStruct((nl, b, XA_WIDTH, nm), F32)
    return pl.pallas_call(
        _memkv_body,
        out_shape=(out, out),
        grid=(nl, b),
        in_specs=[pl.BlockSpec((None, nm, d), lambda l, i: (i, 0, 0)),
                  pl.BlockSpec((None, 1, d), lambda l, i: (l, 0, 0)),
                  pl.BlockSpec((None, 2 * XA_WIDTH, d), lambda l, i: (l, 0, 0)),
                  pl.BlockSpec((None, XA_DIM, 1), lambda l, i: (l, 0, 0))],
        out_specs=(pl.BlockSpec((None, None, XA_WIDTH, nm), lambda l, i: (l, i, 0, 0)),
                   pl.BlockSpec((None, None, XA_WIDTH, nm), lambda l, i: (l, i, 0, 0))),
        compiler_params=_cparams(("parallel", "parallel")),
        name="memory_kv",
    )(mem, mem_norm, w_kv_t, knorm_col)


def _xattn_body(q_ref, mk_ref, mv_ref, qn_ref, bd_ref, o_ref):
    q = _group_rms(q_ref[...], qn_ref[...], bd_ref[...], XA_DIM) * (XA_DIM ** -0.5)
    mkt = mk_ref[...].astype(BF16)
    mvt = mv_ref[...]
    lane = _iota((1, XA_WIDTH), 1) // XA_DIM
    row = _iota((XA_WIDTH, 1), 0) // XA_DIM
    out = None
    for hd in range(XA_HEADS):
        s = jnp.dot(jnp.where(lane == hd, q, 0.0).astype(BF16), mkt, preferred_element_type=F32)
        s = s - jnp.max(s, axis=-1, keepdims=True)
        p = jnp.exp(s)
        p = p / jnp.sum(p, axis=-1, keepdims=True)
        t = lax.dot_general(p.astype(BF16), jnp.where(row == hd, mvt, 0.0).astype(BF16), _NT, preferred_element_type=F32)
        out = t if out is None else out + t
    o_ref[...] = out


def _mem_attend(pp, mk, mv, qn_t, bd64, *, groups, t, tt, row_off, col_blk):
    nt = t // tt
    off = row_off // tt
    return pl.pallas_call(
        _xattn_body,
        out_shape=jax.ShapeDtypeStruct((groups * t, XA_WIDTH), F32),
        grid=(groups, nt),
        in_specs=[pl.BlockSpec((tt, XA_WIDTH), lambda g, i: (off + g * nt + i, col_blk)),
                  pl.BlockSpec((None, N_MEM, XA_WIDTH), lambda g, i: (g, 0, 0)),
                  pl.BlockSpec((None, N_MEM, XA_WIDTH), lambda g, i: (g, 0, 0)),
                  pl.BlockSpec((1, XA_WIDTH), lambda g, i: (0, 0)),
                  pl.BlockSpec((256, 256), lambda g, i: (0, 0))],
        out_specs=pl.BlockSpec((tt, XA_WIDTH), lambda g, i: (g * nt + i, 0)),
        compiler_params=_cparams(("parallel", "parallel")),
        name="mem_attend",
    )(pp, mk, mv, qn_t, bd64)


def _gdn_body(qkv_ref, z_ref, ba_ref, conv0_ref, s0_ref, cw_ref, hp_ref, nw_ref,
              o_ref, sout_ref, xbuf, cbuf, s_scr, *, c):
    j = pl.program_id(1)
    kw = N_HEADS128 * HEAD128

    @pl.when(j == 0)
    def _():
        xbuf[0:8, :] = conv0_ref[...]
        s_scr[...] = s0_ref[...]

    x = qkv_ref[...]
    xbuf[8:8 + c, :] = x
    cw = cw_ref[...]
    conv = x * cw[3:4, :]
    for i in range(1, 4):
        conv = conv + xbuf[8 - i:8 - i + c, :] * cw[3 - i:4 - i, :]
    cbuf[...] = _silu(conv)
    xbuf[0:8, :] = xbuf[c:c + 8, :]

    ba = ba_ref[...]
    hp = hp_ref[...]
    beta_all = _sigmoid(ba)
    g_all = -jnp.exp(hp[0:1, :]) * jax.nn.softplus(ba + hp[1:2, :])
    incl = _tri(c, "incl")
    strict = _tri(c, "strict")
    gcum = _sel_l(_b01(incl), g_all)
    gcum_t = _sel_r(g_all, _b01(_tri(c, "upper_incl")), _TN)
    eg_all = jnp.exp(gcum)
    eye = jnp.where(_iota((c, c), 0) == _iota((c, c), 1), 1.0, 0.0)
    nw = nw_ref[...]

    heads = range(N_HEADS128)
    lo = [hd * HEAD128 for hd in heads]
    q = [cbuf[:, l:l + HEAD128] for l in lo]
    k = [cbuf[:, kw + l:kw + l + HEAD128] for l in lo]
    v = [cbuf[:, 2 * kw + l:2 * kw + l + HEAD128] for l in lo]
    q = [x_ * lax.rsqrt(jnp.sum(x_ * x_, axis=-1, keepdims=True) + EPS) * (HEAD128 ** -0.5) for x_ in q]
    k = [x_ * lax.rsqrt(jnp.sum(x_ * x_, axis=-1, keepdims=True) + EPS) for x_ in k]
    bcol = [beta_all[:, hd:hd + 1] for hd in heads]
    gc = [gcum[:, 8 + hd:9 + hd] for hd in heads]
    egc = [eg_all[:, 8 + hd:9 + hd] for hd in heads]
    gl = [gcum[c - 1:c, 8 + hd:9 + hd] for hd in heads]
    decay = [jnp.exp(jnp.where(incl, gc[hd] - gcum_t[8 + hd:9 + hd, :], -jnp.inf)) for hd in heads]
    kk = [_mm(k[hd], k[hd], _NT) for hd in heads]
    qk = [_mm(q[hd], k[hd], _NT) for hd in heads]
    a = [jnp.where(strict, bcol[hd] * kk[hd] * decay[hd], 0.0) for hd in heads]
    inv = [eye - a_ for a_ in a]
    pw = [_mm3(a_, a_) for a_ in a]
    inv = [inv[hd] + _mm3(inv[hd], pw[hd]) for hd in heads]
    n = 2
    while 2 * n < c:
        pw = [_mm3(p_, p_) for p_ in pw]
        inv = [inv[hd] + _mm3(inv[hd], pw[hd]) for hd in heads]
        n *= 2
    s = [s_scr[hd] for hd in heads]
    sol_v = [_mm3(inv[hd], v[hd] * bcol[hd]) for hd in heads]
    sol_k = [_mm3(inv[hd], k[hd] * (bcol[hd] * egc[hd])) for hd in heads]
    u = [sol_v[hd] - _mm(sol_k[hd], s[hd]) for hd in heads]
    o = [_mm(q[hd] * egc[hd], s[hd]) + _mm(qk[hd] * decay[hd], u[hd]) for hd in heads]
    for hd in heads:
        s_scr[hd] = s[hd] * jnp.exp(gl[hd]) + _mm(k[hd] * jnp.exp(gl[hd] - gc[hd]), u[hd], _TN)
    for hd in heads:
        o_ref[:, lo[hd]:lo[hd] + HEAD128] = _rms_rows(o[hd], nw) * _silu(z_ref[:, lo[hd]:lo[hd] + HEAD128])

    @pl.when(j == pl.num_programs(1) - 1)
    def _():
        sout_ref[...] = s_scr[...]


def _gdn(pp, conv0, s0, cw, hp, nw, *, groups, t, c, row_off):
    nt = t // c
    off = row_off // c
    cc = 3 * N_HEADS128 * HEAD128
    vw = N_HEADS128 * HEAD128
    return pl.pallas_call(
        functools.partial(_gdn_body, c=c),
        out_shape=(jax.ShapeDtypeStruct((groups * t, vw), F32),
                   jax.ShapeDtypeStruct((groups, N_HEADS128, HEAD128, HEAD128), F32)),
        grid=(groups, nt),
        in_specs=[pl.BlockSpec((c, cc), lambda g, i: (off + g * nt + i, 0)),
                  pl.BlockSpec((c, vw), lambda g, i: (off + g * nt + i, cc // vw)),
                  pl.BlockSpec((c, LANES), lambda g, i: (off + g * nt + i, (cc + vw + XA_WIDTH) // LANES)),
                  pl.BlockSpec((None, 8, cc), lambda g, i: (g, 0, 0)),
                  pl.BlockSpec((None, N_HEADS128, HEAD128, HEAD128), lambda g, i: (g, 0, 0, 0)),
                  pl.BlockSpec((4, cc), lambda g, i: (0, 0)),
                  pl.BlockSpec((2, LANES), lambda g, i: (0, 0)),
                  pl.BlockSpec((1, HEAD128), lambda g, i: (0, 0))],
        out_specs=(pl.BlockSpec((c, vw), lambda g, i: (g * nt + i, 0)),
                   pl.BlockSpec((None, N_HEADS128, HEAD128, HEAD128), lambda g, i: (g, 0, 0, 0))),
        scratch_shapes=[pltpu.VMEM((c + 8, cc), F32), pltpu.VMEM((c, cc), F32),
                        pltpu.VMEM((N_HEADS128, HEAD128, HEAD128), F32)],
        compiler_params=_cparams(("parallel", "arbitrary")),
        name="gdn",
    )(pp, pp, pp, conv0, s0, cw, hp, nw)


def _hgrn_body(q_ref, f_ref, i_ref, g_ref, s0_ref, lb_ref, nw_ref, o_ref, sout_ref, st_scr, *, r, c):
    j = pl.program_id(1)

    @pl.when(j == 0)
    def _():
        for hd in range(N_HEADS128):
            st_scr[hd] = s0_ref[hd].T

    nw = nw_ref[...]
    cum_m = _b01(_tri(c, "incl"))
    nsub = c // HG_SUB
    tri8 = _iota((HG_SUB, 1), 0)
    heads = range(N_HEADS128)
    lo = [hd * HEAD128 for hd in heads]
    lb = [lb_ref[:, l:l + HEAD128] for l in lo]
    log_lb = [jnp.log(x_) for x_ in lb]
    log_1m = [jnp.log1p(-x_) for x_ in lb]
    st = [st_scr[hd] for hd in heads]
    for ch in range(r // c):
        r0 = ch * c
        ff = [f_ref[r0:r0 + c, l:l + HEAD128] for l in lo]
        logf = [_logaddexp(log_lb[hd], log_1m[hd] + _log_sigmoid(ff[hd])) for hd in heads]
        k = [(1.0 - lb[hd]) * _sigmoid(-ff[hd]) for hd in heads]
        q = [_silu(q_ref[r0:r0 + c, l:l + HEAD128]) * (HEAD128 ** -0.5) for l in lo]
        v = [i_ref[r0:r0 + c, l:l + HEAD128] for l in lo]
        bc = [_sel_l(cum_m, x_) for x_ in logf]
        o = [_mm(q[hd] * jnp.exp(bc[hd]), st[hd], _NT) for hd in heads]
        blocks = [[o[hd][0:HG_SUB, :]] for hd in heads]
        for sb in range(1, nsub):
            b0 = sb * HG_SUB
            for hd in heads:
                bref = bc[hd][b0 - 1:b0, :]
                qs = q[hd][b0:b0 + HG_SUB, :] * jnp.exp(bc[hd][b0:b0 + HG_SUB, :] - bref)
                ks = k[hd][0:b0, :] * jnp.exp(bref - bc[hd][0:b0, :])
                att = _mm(qs, ks, _NT)
                blocks[hd].append(o[hd][b0:b0 + HG_SUB, :] + _mm(att, v[hd][0:b0, :]))
        for sb in range(nsub):
            b0 = sb * HG_SUB
            for hd in heads:
                qb, bcb = q[hd][b0:b0 + HG_SUB, :], bc[hd][b0:b0 + HG_SUB, :]
                ob = blocks[hd][sb]
                for s_ in range(HG_SUB):
                    w = jnp.exp(bcb - bc[hd][b0 + s_:b0 + s_ + 1, :])
                    col = jnp.sum(qb * k[hd][b0 + s_:b0 + s_ + 1, :] * w, axis=-1, keepdims=True)
                    ob = ob + jnp.where(tri8 >= s_, col, 0.0) * v[hd][b0 + s_:b0 + s_ + 1, :]
                blocks[hd][sb] = ob
        bl = [bc[hd][c - 1:c, :] for hd in heads]
        st = [st[hd] * jnp.exp(bl[hd]) + _mm(v[hd], k[hd] * jnp.exp(bl[hd] - bc[hd]), _TN) for hd in heads]
        for hd in heads:
            ofull = blocks[hd][0] if nsub == 1 else jnp.concatenate(blocks[hd], axis=0)
            o_ref[r0:r0 + c, lo[hd]:lo[hd] + HEAD128] = _rms_rows(ofull, nw) * _silu(g_ref[r0:r0 + c, lo[hd]:lo[hd] + HEAD128])
    for hd in heads:
        st_scr[hd] = st[hd]

    @pl.when(j == pl.num_programs(1) - 1)
    def _():
        for hd in range(N_HEADS128):
            sout_ref[hd] = st_scr[hd].T


def _hgrn(pp, s0, lb, nw, *, groups, t, r, c, row_off):
    nt = t // r
    off = row_off // r
    w = N_HEADS128 * HEAD128
    rows = lambda col: pl.BlockSpec((r, w), lambda g, i: (off + g * nt + i, col))
    return pl.pallas_call(
        functools.partial(_hgrn_body, r=r, c=c),
        out_shape=(jax.ShapeDtypeStruct((groups * t, w), F32),
                   jax.ShapeDtypeStruct((groups, N_HEADS128, HEAD128, HEAD128), F32)),
        grid=(groups, nt),
        in_specs=[rows(0), rows(1), rows(2), rows(3),
                  pl.BlockSpec((None, N_HEADS128, HEAD128, HEAD128), lambda g, i: (g, 0, 0, 0)),
                  pl.BlockSpec((1, w), lambda g, i: (0, 0)),
                  pl.BlockSpec((1, HEAD128), lambda g, i: (0, 0))],
        out_specs=(pl.BlockSpec((r, w), lambda g, i: (g * nt + i, 0)),
                   pl.BlockSpec((None, N_HEADS128, HEAD128, HEAD128), lambda g, i: (g, 0, 0, 0))),
        scratch_shapes=[pltpu.VMEM((N_HEADS128, HEAD128, HEAD128), F32)],
        compiler_params=_cparams(("parallel", "arbitrary")),
        name="hgrn2",
    )(pp, pp, pp, pp, s0, lb, nw)


def _fox_prep_body(q_ref, k_ref, fl_ref, qn_ref, kn_ref, fb_ref, bd_ref, qo_ref, ko_ref, lf_ref, fc_ref, carry):
    j = pl.program_id(1)

    @pl.when(j == 0)
    def _():
        carry[...] = jnp.zeros_like(carry)

    bd = bd_ref[...]
    qo_ref[...] = _group_rms(q_ref[...], qn_ref[...], bd, FOX_DIM)
    ko_ref[...] = _group_rms(k_ref[...], kn_ref[...], bd, FOX_DIM)
    lf = jax.nn.log_sigmoid(fl_ref[...] + fb_ref[...])
    lf_ref[...] = lf
    tr = lf.shape[0]
    fc = _sel_l(_b01(_tri(tr, "incl")), lf) + carry[...]
    fc_ref[...] = fc
    carry[...] = fc[tr - 1:tr, :]


def _fox_prep(pp, qn_t, kn_t, fb, bd64, *, groups, t, tr, row_off):
    nt = t // tr
    off = row_off // tr
    w = FOX_HEADS * FOX_DIM
    n = groups * t
    big = jax.ShapeDtypeStruct((n, w), F32)
    small = jax.ShapeDtypeStruct((n, LANES), F32)
    return pl.pallas_call(
        _fox_prep_body,
        out_shape=(big, big, small, small),
        grid=(groups, nt),
        in_specs=[pl.BlockSpec((tr, w), lambda g, i: (off + g * nt + i, 0)),
                  pl.BlockSpec((tr, w), lambda g, i: (off + g * nt + i, 1)),
                  pl.BlockSpec((tr, LANES), lambda g, i: (off + g * nt + i, (4 * w + XA_WIDTH) // LANES)),
                  pl.BlockSpec((1, w), lambda g, i: (0, 0)),
                  pl.BlockSpec((1, w), lambda g, i: (0, 0)),
                  pl.BlockSpec((1, LANES), lambda g, i: (0, 0)),
                  pl.BlockSpec((256, 256), lambda g, i: (0, 0))],
        out_specs=(pl.BlockSpec((tr, w), lambda g, i: (g * nt + i, 0)),
                   pl.BlockSpec((tr, w), lambda g, i: (g * nt + i, 0)),
                   pl.BlockSpec((tr, LANES), lambda g, i: (g * nt + i, 0)),
                   pl.BlockSpec((tr, LANES), lambda g, i: (g * nt + i, 0))),
        scratch_shapes=[pltpu.VMEM((1, LANES), F32)],
        compiler_params=_cparams(("parallel", "arbitrary")),
        name="fox_prep",
    )(pp, pp, pp, qn_t, kn_t, fb, bd64)


def _fox_flash_body(q_ref, k_ref, v_ref, og_ref, fq_ref, fk_ref, o_ref, m_scr, l_scr, acc_scr, *, tq):
    hp = pl.program_id(1)
    qi = pl.program_id(2)
    lane = _iota((1, LANES), 1)
    q = q_ref[...] * (FOX_DIM ** -0.5)
    fq = fq_ref[...]
    qs, fqs = [], []
    for j in range(2):
        qs.append(jnp.where((lane // FOX_DIM) == j, q, 0.0).astype(BF16))
        fqs.append(jnp.sum(jnp.where(lane == 2 * hp + j, fq, 0.0), axis=-1, keepdims=True))
        m_scr[j] = jnp.full((tq, 1), -jnp.inf, F32)
        l_scr[j] = jnp.zeros((tq, 1), F32)
        acc_scr[j] = jnp.zeros((tq, LANES), F32)

    def block(ki, masked):
        k0 = pl.multiple_of(ki * tq, tq)
        kb = k_ref[pl.ds(k0, tq), :].astype(BF16)
        vb = v_ref[pl.ds(k0, tq), :].astype(BF16)
        two = range(2)
        fk = [fk_ref[pl.ds(2 * hp + j, 1), pl.ds(k0, tq)] for j in two]
        s = [lax.dot_general(qs[j], kb, _NT, preferred_element_type=F32) + fqs[j] - fk[j] for j in two]
        if masked:
            s = [jnp.where(_tri(tq, "incl"), s_, -jnp.inf) for s_ in s]
        m_old = [m_scr[j] for j in two]
        m_new = [jnp.maximum(m_old[j], jnp.max(s[j], axis=-1, keepdims=True)) for j in two]
        alpha = [jnp.exp(m_old[j] - m_new[j]) for j in two]
        p = [jnp.exp(s[j] - m_new[j]) for j in two]
        pv = [jnp.dot(p[j].astype(BF16), vb, preferred_element_type=F32) for j in two]
        for j in two:
            l_scr[j] = alpha[j] * l_scr[j] + jnp.sum(p[j], axis=-1, keepdims=True)
            acc_scr[j] = alpha[j] * acc_scr[j] + pv[j]
            m_scr[j] = m_new[j]

    def body(ki, carry):
        block(ki, False)
        return carry

    lax.fori_loop(0, qi, body, 0)
    block(qi, True)
    o0 = acc_scr[0] / l_scr[0]
    o1 = acc_scr[1] / l_scr[1]
    o_ref[...] = jnp.where(lane < FOX_DIM, o0, o1) * _sigmoid(og_ref[...])


def _fox_flash(qn, kn, pp, fc, fct, *, groups, t, tq):
    nq = t // tq
    w = FOX_HEADS * FOX_DIM
    hp_n = w // LANES
    vblk = 2 * w // LANES
    gblk = 3 * w // LANES
    return pl.pallas_call(
        functools.partial(_fox_flash_body, tq=tq),
        out_shape=jax.ShapeDtypeStruct((groups * t, w), F32),
        grid=(groups, hp_n, nq),
        in_specs=[pl.BlockSpec((tq, LANES), lambda g, h, i: (g * nq + i, h)),
                  pl.BlockSpec((t, LANES), lambda g, h, i: (g, h)),
                  pl.BlockSpec((t, LANES), lambda g, h, i: (g, vblk + h)),
                  pl.BlockSpec((tq, LANES), lambda g, h, i: (g * nq + i, gblk + h)),
                  pl.BlockSpec((tq, LANES), lambda g, h, i: (g * nq + i, 0)),
                  pl.BlockSpec((None, FOX_HEADS, t), lambda g, h, i: (g, 0, 0))],
        out_specs=pl.BlockSpec((tq, LANES), lambda g, h, i: (g * nq + i, h)),
        scratch_shapes=[pltpu.VMEM((2, tq, 1), F32), pltpu.VMEM((2, tq, 1), F32), pltpu.VMEM((2, tq, LANES), F32)],
        compiler_params=_cparams(("parallel", "parallel", "arbitrary")),
        name="fox_flash",
    )(qn, kn, pp, pp, fc, fct)


def _fox_sample_body(pt_ref, *refs, gp, tnew):
    del pt_ref
    k_refs = refs[0:gp]
    v_refs = refs[gp:2 * gp]
    lf_refs = refs[2 * gp:3 * gp]
    q_ref, kn_ref, vn_ref, og_ref, lfn_ref, o_ref, q_scr, m_scr, l_scr, acc, carry = refs[3 * gp:]
    pg = pl.program_id(1)

    @pl.when(pg == 0)
    def _():
        qs = q_ref[...] * (FOX_DIM ** -0.5)
        for hd in range(FOX_HEADS):
            q_scr[hd] = qs[:, hd * FOX_DIM:(hd + 1) * FOX_DIM]
        m_scr[...] = jnp.full(m_scr.shape, -jnp.inf, F32)
        l_scr[...] = jnp.zeros_like(l_scr)
        acc[...] = jnp.zeros_like(acc)
        carry[...] = jnp.zeros_like(carry)

    fnew = _sel_l(_b01(_tri(tnew, "incl")), lfn_ref[...])

    heads = range(FOX_HEADS)

    def update(s, pv_fn):
        m_old = [m_scr[hd] for hd in heads]
        m_new = [jnp.maximum(m_old[hd], jnp.max(s[hd], axis=-1, keepdims=True)) for hd in heads]
        alpha = [jnp.exp(m_old[hd] - m_new[hd]) for hd in heads]
        p = [jnp.exp(s[hd] - m_new[hd]) for hd in heads]
        pv = [pv_fn(hd, p[hd].astype(BF16)) for hd in heads]
        for hd in heads:
            l_scr[hd] = alpha[hd] * l_scr[hd] + jnp.sum(p[hd], axis=-1, keepdims=True)
            acc[hd] = alpha[hd] * acc[hd] + pv[hd]
            m_scr[hd] = m_new[hd]

    later = _b01(_tri(PAGE, "strict"))
    rests = []
    run = carry[...]
    for j in range(gp):
        lft = lf_refs[j][...]
        rests.append(_sel_r(lft, later) + run)
        run = run + jnp.sum(lft, axis=-1, keepdims=True)
    carry[...] = run

    qh = [q_scr[hd].astype(BF16) for hd in heads]
    s = [jnp.concatenate(
        [jnp.dot(qh[hd], k_refs[j][hd].astype(BF16), preferred_element_type=F32) + rests[j][hd:hd + 1, :]
         for j in range(gp)], axis=1) + fnew[:, hd:hd + 1] for hd in heads]

    def pv_past(hd, p):
        parts = [lax.dot_general(p[:, j * PAGE:(j + 1) * PAGE], v_refs[j][hd].astype(BF16), _NT,
                                 preferred_element_type=F32) for j in range(gp)]
        while len(parts) > 1:
            parts = [parts[i] + parts[i + 1] for i in range(0, len(parts), 2)]
        return parts[0]

    update(s, pv_past)

    @pl.when(pg == pl.num_programs(1) - 1)
    def _():
        fnew_t = fnew.T
        causal = _tri(tnew, "incl")
        lo = [hd * FOX_DIM for hd in heads]
        qh = [q_scr[hd].astype(BF16) for hd in heads]
        s = [lax.dot_general(qh[hd], kn_ref[:, lo[hd]:lo[hd] + FOX_DIM].astype(BF16), _NT, preferred_element_type=F32)
             + fnew[:, hd:hd + 1] - fnew_t[hd:hd + 1, :] for hd in heads]
        s = [jnp.where(causal, s_, -jnp.inf) for s_ in s]
        update(s, lambda hd, p: jnp.dot(p, vn_ref[:, lo[hd]:lo[hd] + FOX_DIM].astype(BF16), preferred_element_type=F32))
        o_ref[...] = jnp.concatenate([acc[hd] / l_scr[hd] for hd in heads], axis=1) * _sigmoid(og_ref[...])


def _fox_sample(page_table, ckt, cvt, clft, qn, kn, pp, lfn, *, tnew, row_off, gp):
    ns, npages = page_table.shape
    w = FOX_HEADS * FOX_DIM
    ngrp = npages // gp
    off = row_off // tnew

    def page_map(j, nd):
        return lambda s, g, pt: (pt[s, npages - 1 - (g * gp + j)],) + (0,) * nd

    kv_specs = [pl.BlockSpec((None, FOX_HEADS, FOX_DIM, PAGE), page_map(j, 3)) for j in range(gp)]
    lf_specs = [pl.BlockSpec((None, FOX_HEADS, PAGE), page_map(j, 2)) for j in range(gp)]
    row = lambda s, g, pt: (s, 0)
    grid_spec = pltpu.PrefetchScalarGridSpec(
        num_scalar_prefetch=1,
        grid=(ns, ngrp),
        in_specs=kv_specs + kv_specs + lf_specs + [
            pl.BlockSpec((tnew, w), row),
            pl.BlockSpec((tnew, w), row),
            pl.BlockSpec((tnew, w), lambda s, g, pt: (off + s, 2)),
            pl.BlockSpec((tnew, w), lambda s, g, pt: (off + s, 3)),
            pl.BlockSpec((tnew, LANES), row)],
        out_specs=pl.BlockSpec((tnew, w), row),
        scratch_shapes=[pltpu.VMEM((FOX_HEADS, tnew, FOX_DIM), F32), pltpu.VMEM((FOX_HEADS, tnew, 1), F32),
                        pltpu.VMEM((FOX_HEADS, tnew, 1), F32), pltpu.VMEM((FOX_HEADS, tnew, FOX_DIM), F32),
                        pltpu.VMEM((FOX_HEADS, 1), F32)],
    )
    return pl.pallas_call(
        functools.partial(_fox_sample_body, gp=gp, tnew=tnew),
        out_shape=jax.ShapeDtypeStruct((ns * tnew, w), F32),
        grid_spec=grid_spec,
        compiler_params=_cparams(("parallel", "arbitrary")),
        name="fox_sample",
    )(page_table, *([ckt] * gp), *([cvt] * gp), *([clft] * gp), qn, kn, pp, pp, lfn)


def _cmlp_body(u_ref, v_ref, lg_ref, lbias_ref, ws_ref, bsb_ref, o_ref, vo_ref):
    u = jax.nn.gelu(u_ref[...], approximate=True)
    z = jax.nn.gelu(v_ref[...], approximate=True)
    mu = jnp.mean(z, axis=-1, keepdims=True)
    zc = z - mu
    v = zc * lax.rsqrt(jnp.mean(zc * zc, axis=-1, keepdims=True) + EPS) * lg_ref[...] + lbias_ref[...]
    vo_ref[...] = v
    tril = _tri(CM_CHUNK, "incl")
    gd = v.shape[1] // CM_GROUPS
    for g in range(CM_GROUPS):
        wm = jnp.where(tril, ws_ref[g], 0.0)
        mixed = _mm(wm, v[:, g * gd:(g + 1) * gd]) + bsb_ref[g]
        o_ref[:, g * gd:(g + 1) * gd] = u[:, g * gd:(g + 1) * gd] * mixed


def _chunk_mlp(pp, ln_g, ln_b, ws, bsb, *, nrows, row_off):
    w = ln_g.shape[1]
    off = row_off // CM_CHUNK
    out = jax.ShapeDtypeStruct((nrows, w), F32)
    const2 = lambda i: (0, 0)
    const3 = lambda i: (0, 0, 0)
    return pl.pallas_call(
        _cmlp_body,
        out_shape=(out, out),
        grid=(nrows // CM_CHUNK,),
        in_specs=[pl.BlockSpec((CM_CHUNK, w), lambda i: (off + i, 0)),
                  pl.BlockSpec((CM_CHUNK, w), lambda i: (off + i, 1)),
                  pl.BlockSpec((1, w), const2), pl.BlockSpec((1, w), const2),
                  pl.BlockSpec((CM_GROUPS, CM_CHUNK, CM_CHUNK), const3),
                  pl.BlockSpec((CM_GROUPS, CM_CHUNK, w // CM_GROUPS), const3)],
        out_specs=(pl.BlockSpec((CM_CHUNK, w), lambda i: (i, 0)),
                   pl.BlockSpec((CM_CHUNK, w), lambda i: (i, 0))),
        compiler_params=_cparams(("parallel",)),
        name="chunk_mlp",
    )(pp, pp, ln_g, ln_b, ws, bsb)


def _pad_cols(w, mult):
    pad = (-w.shape[1]) % mult
    return jnp.pad(w, ((0, 0), (0, pad))) if pad else w


def _lane_row(v, start=0):
    return jnp.zeros((1, LANES), F32).at[0, start:start + v.shape[0]].set(v.astype(F32))


def kernel(x_prompt, x_sample, mem_prompt, state_a_conv, state_a_ssm, state_b_ssm, cache_c_k, cache_c_v, cache_c_logf, cache_mem_k, cache_mem_v, page_table, norm_mix, w_out, norm_mlp, w_up, w_down, mem_norm, w_mem_kv, xa_qnorm, xa_knorm, w_in_a, a_conv_w, a_log, a_dt_bias, a_norm_w, w_in_b, hg_lb, b_norm_w, w_in_c, c_fbias, c_qnorm, c_knorm, w_in_d, d_ln_g, d_ln_b, d_ws, d_bs):
    bp, seq, d = x_prompt.shape
    ds, dseq, _ = x_sample.shape
    depth = norm_mix.shape[0]
    np_rows, ns_rows = bp * seq, ds * dseq
    n = np_rows + ns_rows
    tm = 640 if n % 640 == 0 else 128
    w1024 = N_HEADS128 * HEAD128
    cc = 3 * w1024

    h = jnp.concatenate([x_prompt.reshape(np_rows, d), x_sample.reshape(ns_rows, d)], axis=0)
    bd64 = (jnp.arange(256)[:, None] // 64 == jnp.arange(256)[None, :] // 64).astype(BF16)

    mk_all, mv_all = _memory_kv(mem_prompt, mem_norm[:, None, :], jnp.swapaxes(w_mem_kv, 1, 2).astype(BF16),
                                xa_knorm[:, :, None])

    lb_w = jax.nn.softmax(hg_lb.astype(F32), axis=0)
    lower_bounds = jnp.cumsum(lb_w, axis=0) - lb_w[0]

    outs = {}
    for l in range(depth):
        kind, j = l % 4, l // 4
        if kind == 0:
            wi = w_in_a[j]
            w_packed = jnp.concatenate([wi[:, :cc + w1024], wi[:, cc + w1024 + 16:], _pad_cols(wi[:, cc + w1024:cc + w1024 + 16], LANES)], axis=1)
            xq_blk = (cc + w1024) // XA_WIDTH
        elif kind == 1:
            w_packed = w_in_b[j]
            xq_blk = 4 * w1024 // XA_WIDTH
        elif kind == 2:
            wi = w_in_c[j]
            w_packed = jnp.concatenate([wi[:, :4 * w1024], wi[:, 4 * w1024 + 16:], _pad_cols(wi[:, 4 * w1024:4 * w1024 + 16], LANES)], axis=1)
            xq_blk = 4 * w1024 // XA_WIDTH
        else:
            w_packed = w_in_d[j]
            xq_blk = 2 * w1024 // XA_WIDTH
        pp = _norm_proj(h, norm_mix[l][None, :], w_packed.astype(BF16), tm=tm)

        qn_t = jnp.tile(xa_qnorm[l], XA_HEADS)[None, :]
        xo_p = _mem_attend(pp, mk_all[l], mv_all[l], qn_t, bd64, groups=bp, t=seq, tt=512, row_off=0, col_blk=xq_blk)
        to_hdn = lambda a: jnp.transpose(a, (0, 2, 3, 1)).reshape(ds, XA_WIDTH, N_MEM)
        xo_s = _mem_attend(pp, to_hdn(cache_mem_k[l]), to_hdn(cache_mem_v[l]),
                           qn_t, bd64, groups=ds, t=dseq, tt=dseq, row_off=np_rows, col_blk=xq_blk)

        if kind == 0:
            hp = jnp.concatenate([_lane_row(a_log[j], 8), _lane_row(a_dt_bias[j], 8)], axis=0)
            nw = a_norm_w[j][None, :]
            conv0_p = jnp.zeros((bp, 8, cc), F32)
            conv0_s = jnp.pad(state_a_conv[j], ((0, 0), (5, 0), (0, 0)))
            mo_p, st_p = _gdn(pp, conv0_p, jnp.zeros((bp, N_HEADS128, HEAD128, HEAD128), F32), a_conv_w[j], hp, nw,
                              groups=bp, t=seq, c=GDN_CHUNK, row_off=0)
            mo_s, st_s = _gdn(pp, conv0_s, state_a_ssm[j], a_conv_w[j], hp, nw,
                              groups=ds, t=dseq, c=dseq, row_off=np_rows)
            outs.setdefault("a_conv_p", []).append(pp[:np_rows, :cc].reshape(bp, seq, cc)[:, seq - 3:])
            outs.setdefault("a_conv_s", []).append(pp[np_rows:, :cc].reshape(ds, dseq, cc)[:, dseq - 3:])
            outs.setdefault("a_ssm_p", []).append(st_p)
            outs.setdefault("a_ssm_s", []).append(st_s)
        elif kind == 1:
            lb = lower_bounds[l][None, :]
            nw = b_norm_w[j][None, :]
            mo_p, st_p = _hgrn(pp, jnp.zeros((bp, N_HEADS128, HEAD128, HEAD128), F32), lb, nw,
                               groups=bp, t=seq, r=128, c=HG_CHUNK, row_off=0)
            mo_s, st_s = _hgrn(pp, state_b_ssm[j], lb, nw, groups=ds, t=dseq, r=dseq, c=dseq, row_off=np_rows)
            outs.setdefault("b_ssm_p", []).append(st_p)
            outs.setdefault("b_ssm_s", []).append(st_s)
        elif kind == 2:
            qn_f = jnp.tile(c_qnorm[j], FOX_HEADS)[None, :]
            kn_f = jnp.tile(c_knorm[j], FOX_HEADS)[None, :]
            fb = _lane_row(c_fbias[j])
            q_p, k_p, lf_p, fc_p = _fox_prep(pp, qn_f, kn_f, fb, bd64, groups=bp, t=seq, tr=256, row_off=0)
            q_s, k_s, lf_s, _ = _fox_prep(pp, qn_f, kn_f, fb, bd64, groups=ds, t=dseq, tr=dseq, row_off=np_rows)
            fct = jnp.swapaxes(fc_p[:, :FOX_HEADS].reshape(bp, seq, FOX_HEADS), 1, 2)
            mo_p = _fox_flash(q_p, k_p, pp, fc_p, fct, groups=bp, t=seq, tq=FOX_TQ)
            pos_minor = lambda a: jnp.transpose(a, (0, 2, 3, 1))
            mo_s = _fox_sample(page_table, pos_minor(cache_c_k[j]), pos_minor(cache_c_v[j]),
                               jnp.swapaxes(cache_c_logf[j], 1, 2), q_s, k_s, pp, lf_s,
                               tnew=dseq, row_off=np_rows, gp=PAGES_PER_STEP)
            v_all = pp[:, 2 * w1024:3 * w1024]
            outs.setdefault("c_k_p", []).append(k_p.reshape(bp, seq, FOX_HEADS, FOX_DIM))
            outs.setdefault("c_v_p", []).append(v_all[:np_rows].reshape(bp, seq, FOX_HEADS, FOX_DIM))
            outs.setdefault("c_lf_p", []).append(lf_p[:, :FOX_HEADS].reshape(bp, seq, FOX_HEADS))
            outs.setdefault("c_k_s", []).append(k_s.reshape(ds, dseq, FOX_HEADS, FOX_DIM))
            outs.setdefault("c_v_s", []).append(v_all[np_rows:].reshape(ds, dseq, FOX_HEADS, FOX_DIM))
            outs.setdefault("c_lf_s", []).append(lf_s[:, :FOX_HEADS].reshape(ds, dseq, FOX_HEADS))
        else:
            bsb = jnp.broadcast_to(d_bs[j][:, :, None], (CM_GROUPS, CM_CHUNK, w1024 // CM_GROUPS))
            lg, lbias = d_ln_g[j][None, :], d_ln_b[j][None, :]
            mo_p, _ = _chunk_mlp(pp, lg, lbias, d_ws[j], bsb, nrows=np_rows, row_off=0)
            pps = jnp.pad(pp[np_rows:, :2 * w1024].reshape(ds, dseq, 2 * w1024), ((0, 0), (0, CM_CHUNK - dseq), (0, 0)))
            mo_s, v_s = _chunk_mlp(pps.reshape(ds * CM_CHUNK, 2 * w1024), lg, lbias, d_ws[j], bsb, nrows=ds * CM_CHUNK, row_off=0)
            mo_s = mo_s.reshape(ds, CM_CHUNK, w1024)[:, :dseq].reshape(ns_rows, w1024)
            outs.setdefault("d_v_s", []).append(v_s.reshape(ds, CM_CHUNK, w1024)[:, :dseq])

        mo = jnp.concatenate([mo_p, mo_s], axis=0)
        xo = jnp.concatenate([xo_p, xo_s], axis=0)
        wo = w_out[l].astype(BF16)
        h = _out_mlp(h, mo, xo, wo[:w1024], wo[w1024:], norm_mlp[l][None, :], w_up[l].astype(BF16), w_down[l].astype(BF16), tm=tm)

    st = lambda name: jnp.stack(outs[name])
    mem_out = lambda a: jnp.transpose(a.reshape(depth, bp, XA_HEADS, XA_DIM, N_MEM), (0, 1, 4, 2, 3))
    return (h[:np_rows].reshape(bp, seq, d), h[np_rows:].reshape(ds, dseq, d),
            st("a_conv_p"), st("a_conv_s"), st("a_ssm_p"), st("a_ssm_s"),
            st("b_ssm_p"), st("b_ssm_s"),
            st("c_k_p"), st("c_v_p"), st("c_lf_p"), st("c_k_s"), st("c_v_s"), st("c_lf_s"),
            st("d_v_s"), mem_out(mk_all), mem_out(mv_all))
```

```python
import functools
import math

import jax
import jax.numpy as jnp
from jax import lax
from jax.experimental import pallas as pl
from jax.experimental.pallas import tpu as pltpu

F32 = jnp.float32
BF16 = jnp.bfloat16
EPS = 1e-6
LOG2E = 1.4426950408889634

LANES = 128
SUBLANES = 8
VMEM_LIMIT = 56 * 1024 * 1024

XA_HEADS, XA_DIM, N_MEM = 4, 64, 256
XA_WIDTH = XA_HEADS * XA_DIM
HEAD128 = 128
N_HEADS128 = 8
FOX_DIM, FOX_HEADS = 64, 16
PAGE = 128
CM_CHUNK, CM_GROUPS = 128, 8
GDN_CHUNK, HG_CHUNK, HG_SUB = 64, 32, 8
FOX_TQ = 512
FOX_RG = 128
PAGES_PER_STEP = 8


def _cparams(sem):
    return pltpu.CompilerParams(dimension_semantics=sem, vmem_limit_bytes=VMEM_LIMIT)


def _iota(shape, dim):
    return lax.broadcasted_iota(jnp.int32, shape, dim)


def _split3(x):
    hi = x.astype(BF16)
    r = x - hi.astype(F32)
    mid = r.astype(BF16)
    lo = (r - mid.astype(F32)).astype(BF16)
    return hi, mid, lo


def _split2(x):
    hi = x.astype(BF16)
    lo = (x - hi.astype(F32)).astype(BF16)
    return hi, lo


_NN = (((1,), (0,)), ((), ()))
_NT = (((1,), (1,)), ((), ()))
_TN = (((0,), (0,)), ((), ()))


def _mm(a, b, dims=_NN):
    return lax.dot_general(a.astype(BF16), b.astype(BF16), dims, preferred_element_type=F32)


def _sel_l(m, x, dims=_NN):
    out = None
    for p in _split3(x):
        t = lax.dot_general(m, p, dims, preferred_element_type=F32)
        out = t if out is None else out + t
    return out


def _sel_r(x, m, dims=_NN):
    out = None
    for p in _split3(x):
        t = lax.dot_general(p, m, dims, preferred_element_type=F32)
        out = t if out is None else out + t
    return out


def _mm3(a, b):
    ah, al = _split2(a)
    bh, bl = _split2(b)
    d = functools.partial(jnp.dot, preferred_element_type=F32)
    return d(ah, bh) + d(ah, bl) + d(al, bh)


def _sigmoid(x):
    return jax.nn.sigmoid(x)


def _silu(x):
    return x * jax.nn.sigmoid(x)


def _log_sigmoid(x):
    return jnp.minimum(x, 0.0) - jnp.log1p(jnp.exp(-jnp.abs(x)))


def _logaddexp(a, b):
    return jnp.maximum(a, b) + jnp.log1p(jnp.exp(-jnp.abs(a - b)))


def _rms_rows(x, g):
    return x * lax.rsqrt(jnp.mean(x * x, axis=-1, keepdims=True) + EPS) * g


def _group_rms(x, gain, bd, gsz):
    w = x.shape[1]
    x2 = x * x
    parts = []
    for c in range(0, w, 256):
        parts.append(_sel_r(x2[:, c:c + 256], bd))
    ms = (parts[0] if len(parts) == 1 else jnp.concatenate(parts, axis=1)) * (1.0 / gsz)
    return x * lax.rsqrt(ms + EPS) * gain


def _tri(n, kind):
    r, c = _iota((n, n), 0), _iota((n, n), 1)
    if kind == "incl":
        return r >= c
    if kind == "strict":
        return r > c
    if kind == "upper_incl":
        return r <= c
    if kind == "upper_strict":
        return r < c
    raise ValueError(kind)


def _b01(mask):
    return jnp.where(mask, 1.0, 0.0).astype(BF16)


def _proj_body(x_ref, g_ref, w_ref, o_ref, *, tn):
    xn = _rms_rows(x_ref[...], g_ref[...]).astype(BF16)
    wp = w_ref.shape[1]
    for c in range(0, wp, tn):
        e = min(c + tn, wp)
        o_ref[:, c:e] = jnp.dot(xn, w_ref[:, c:e], preferred_element_type=F32)


def _norm_proj(h, g, w, *, tm, tn=512):
    n, d = h.shape
    wp = w.shape[1]
    return pl.pallas_call(
        functools.partial(_proj_body, tn=tn),
        out_shape=jax.ShapeDtypeStruct((n, wp), F32),
        grid=(n // tm,),
        in_specs=[pl.BlockSpec((tm, d), lambda i: (i, 0)),
                  pl.BlockSpec((1, d), lambda i: (0, 0)),
                  pl.BlockSpec((d, wp), lambda i: (0, 0), pipeline_mode=pl.Buffered(1))],
        out_specs=pl.BlockSpec((tm, wp), lambda i: (i, 0)),
        compiler_params=_cparams(("parallel",)),
        name="norm_proj",
    )(h, g, w)


def _out_mlp_body(h_ref, mo_ref, xo_ref, wo1_ref, wo2_ref, g_ref, wu_ref, wd_ref, o_ref, *, tf):
    h2 = (h_ref[...]
          + jnp.dot(mo_ref[...].astype(BF16), wo1_ref[...], preferred_element_type=F32)
          + jnp.dot(xo_ref[...].astype(BF16), wo2_ref[...], preferred_element_type=F32))
    xn = _rms_rows(h2, g_ref[...]).astype(BF16)
    acc = None
    dff = wu_ref.shape[1]
    for c in range(0, dff, tf):
        up = jnp.dot(xn, wu_ref[:, c:c + tf], preferred_element_type=F32)
        act = jnp.square(jnp.maximum(up, 0.0)).astype(BF16)
        t = jnp.dot(act, wd_ref[c:c + tf, :], preferred_element_type=F32)
        acc = t if acc is None else acc + t
    o_ref[...] = h2 + acc


def _out_mlp(h, mo, xo, wo1, wo2, g, wu, wd, *, tm, tf=512):
    n, d = h.shape
    dff = wu.shape[1]
    xw = xo.shape[1]
    const = lambda i: (0, 0)
    one = pl.Buffered(1)
    return pl.pallas_call(
        functools.partial(_out_mlp_body, tf=tf),
        out_shape=jax.ShapeDtypeStruct((n, d), F32),
        grid=(n // tm,),
        in_specs=[pl.BlockSpec((tm, d), lambda i: (i, 0)),
                  pl.BlockSpec((tm, d), lambda i: (i, 0)),
                  pl.BlockSpec((tm, xw), lambda i: (i, 0)),
                  pl.BlockSpec((d, d), const, pipeline_mode=one),
                  pl.BlockSpec((xw, d), const, pipeline_mode=one),
                  pl.BlockSpec((1, d), const),
                  pl.BlockSpec((d, dff), const, pipeline_mode=one),
                  pl.BlockSpec((dff, d), const, pipeline_mode=one)],
        out_specs=pl.BlockSpec((tm, d), lambda i: (i, 0)),
        compiler_params=_cparams(("parallel",)),
        name="out_mlp",
    )(h, mo, xo, wo1, wo2, g, wu, wd)


def _memkv_body(x_ref, g_ref, wt_ref, kn_ref, k_ref, v_ref):
    xn = _rms_rows(x_ref[...], g_ref[...]).astype(BF16)
    kvt = lax.dot_general(wt_ref[...], xn, _NT, preferred_element_type=F32)
    kn = kn_ref[...]
    for hd in range(XA_HEADS):
        kt = kvt[hd * XA_DIM:(hd + 1) * XA_DIM, :]
        ms = jnp.mean(kt * kt, axis=0, keepdims=True)
        k_ref[hd * XA_DIM:(hd + 1) * XA_DIM, :] = kt * lax.rsqrt(ms + EPS) * kn
    v_ref[...] = kvt[XA_WIDTH:, :]


def _memory_kv(mem, mem_norm, w_kv_t, knorm_col):
    b, nm, d = mem.shape
    nl = w_kv_t.shape[0]
    out = jax.ShapeDtypeStruct((nl, b, XA_WIDTH, nm), F32)
    return pl.pallas_call(
        _memkv_body,
        out_shape=(out, out),
        grid=(nl, b),
        in_specs=[pl.BlockSpec((None, nm, d), lambda l, i: (i, 0, 0)),
                  pl.BlockSpec((None, 1, d), lambda l, i: (l, 0, 0)),
                  pl.BlockSpec((None, 2 * XA_WIDTH, d), lambda l, i: (l, 0, 0)),
                  pl.BlockSpec((None, XA_DIM, 1), lambda l, i: (l, 0, 0))],
        out_specs=(pl.BlockSpec((None, None, XA_WIDTH, nm), lambda l, i: (l, i, 0, 0)),
                   pl.BlockSpec((None, None, XA_WIDTH, nm), lambda l, i: (l, i, 0, 0))),
        compiler_params=_cparams(("parallel", "parallel")),
        name="memory_kv",
    )(mem, mem_norm, w_kv_t, knorm_col)


def _xattn_body(q_ref, mk_ref, mv_ref, qn_ref, bd_ref, o_ref):
    q = _group_rms(q_ref[...], qn_ref[...], bd_ref[...], XA_DIM) * (XA_DIM ** -0.5)
    mkt = mk_ref[...].astype(BF16)
    mvt = mv_ref[...]
    lane = _iota((1, XA_WIDTH), 1) // XA_DIM
    row = _iota((XA_WIDTH, 1), 0) // XA_DIM
    out = None
    for hd in range(XA_HEADS):
        s = jnp.dot(jnp.where(lane == hd, q, 0.0).astype(BF16), mkt, preferred_element_type=F32)
        s = s - jnp.max(s, axis=-1, keepdims=True)
        p = jnp.exp(s)
        p = p / jnp.sum(p, axis=-1, keepdims=True)
        t = lax.dot_general(p.astype(BF16), jnp.where(row == hd, mvt, 0.0).astype(BF16), _NT, preferred_element_type=F32)
        out = t if out is None else out + t
    o_ref[...] = out


def _mem_attend(pp, mk, mv, qn_t, bd64, *, groups, t, tt, row_off, col_blk):
    nt = t // tt
    off = row_off // tt
    return pl.pallas_call(
        _xattn_body,
        out_shape=jax.ShapeDtypeStruct((groups * t, XA_WIDTH), F32),
        grid=(groups, nt),
        in_specs=[pl.BlockSpec((tt, XA_WIDTH), lambda g, i: (off + g * nt + i, col_blk)),
                  pl.BlockSpec((None, N_MEM, XA_WIDTH), lambda g, i: (g, 0, 0)),
                  pl.BlockSpec((None, N_MEM, XA_WIDTH), lambda g, i: (g, 0, 0)),
                  pl.BlockSpec((1, XA_WIDTH), lambda g, i: (0, 0)),
                  pl.BlockSpec((256, 256), lambda g, i: (0, 0))],
        out_specs=pl.BlockSpec((tt, XA_WIDTH), lambda g, i: (g * nt + i, 0)),
        compiler_params=_cparams(("parallel", "parallel")),
        name="mem_attend",
    )(pp, mk, mv, qn_t, bd64)


def _gdn_body(qkv_ref, z_ref, ba_ref, conv0_ref, s0_ref, cw_ref, hp_ref, nw_ref,
              o_ref, sout_ref, xbuf, cbuf, s_scr, *, c):
    j = pl.program_id(1)
    kw = N_HEADS128 * HEAD128

    @pl.when(j == 0)
    def _():
        xbuf[0:8, :] = conv0_ref[...]
        s_scr[...] = s0_ref[...]

    x = qkv_ref[...]
    xbuf[8:8 + c, :] = x
    cw = cw_ref[...]
    conv = x * cw[3:4, :]
    for i in range(1, 4):
        conv = conv + xbuf[8 - i:8 - i + c, :] * cw[3 - i:4 - i, :]
    cbuf[...] = _silu(conv)
    xbuf[0:8, :] = xbuf[c:c + 8, :]

    ba = ba_ref[...]
    hp = hp_ref[...]
    beta_all = _sigmoid(ba)
    g_all = -jnp.exp(hp[0:1, :]) * jax.nn.softplus(ba + hp[1:2, :])
    incl = _tri(c, "incl")
    strict = _tri(c, "strict")
    gcum = _sel_l(_b01(incl), g_all)
    gcum_t = _sel_r(g_all, _b01(_tri(c, "upper_incl")), _TN)
    eg_all = jnp.exp(gcum)
    eye = jnp.where(_iota((c, c), 0) == _iota((c, c), 1), 1.0, 0.0)
    nw = nw_ref[...]

    heads = range(N_HEADS128)
    lo = [hd * HEAD128 for hd in heads]
    q = [cbuf[:, l:l + HEAD128] for l in lo]
    k = [cbuf[:, kw + l:kw + l + HEAD128] for l in lo]
    v = [cbuf[:, 2 * kw + l:2 * kw + l + HEAD128] for l in lo]
    q = [x_ * lax.rsqrt(jnp.sum(x_ * x_, axis=-1, keepdims=True) + EPS) * (HEAD128 ** -0.5) for x_ in q]
    k = [x_ * lax.rsqrt(jnp.sum(x_ * x_, axis=-1, keepdims=True) + EPS) for x_ in k]
    bcol = [beta_all[:, hd:hd + 1] for hd in heads]
    gc = [gcum[:, 8 + hd:9 + hd] for hd in heads]
    egc = [eg_all[:, 8 + hd:9 + hd] for hd in heads]
    gl = [gcum[c - 1:c, 8 + hd:9 + hd] for hd in heads]
    decay = [jnp.exp(jnp.where(incl, gc[hd] - gcum_t[8 + hd:9 + hd, :], -jnp.inf)) for hd in heads]
    kk = [_mm(k[hd], k[hd], _NT) for hd in heads]
    qk = [_mm(q[hd], k[hd], _NT) for hd in heads]
    a = [jnp.where(strict, bcol[hd] * kk[hd] * decay[hd], 0.0) for hd in heads]
    inv = [eye - a_ for a_ in a]
    pw = [_mm3(a_, a_) for a_ in a]
    inv = [inv[hd] + _mm3(inv[hd], pw[hd]) for hd in heads]
    n = 2
    while 2 * n < c:
        pw = [_mm3(p_, p_) for p_ in pw]
        inv = [inv[hd] + _mm3(inv[hd], pw[hd]) for hd in heads]
        n *= 2
    s = [s_scr[hd] for hd in heads]
    sol_v = [_mm3(inv[hd], v[hd] * bcol[hd]) for hd in heads]
    sol_k = [_mm3(inv[hd], k[hd] * (bcol[hd] * egc[hd])) for hd in heads]
    u = [sol_v[hd] - _mm(sol_k[hd], s[hd]) for hd in heads]
    o = [_mm(q[hd] * egc[hd], s[hd]) + _mm(qk[hd] * decay[hd], u[hd]) for hd in heads]
    for hd in heads:
        s_scr[hd] = s[hd] * jnp.exp(gl[hd]) + _mm(k[hd] * jnp.exp(gl[hd] - gc[hd]), u[hd], _TN)
    for hd in heads:
        o_ref[:, lo[hd]:lo[hd] + HEAD128] = _rms_rows(o[hd], nw) * _silu(z_ref[:, lo[hd]:lo[hd] + HEAD128])

    @pl.when(j == pl.num_programs(1) - 1)
    def _():
        sout_ref[...] = s_scr[...]


def _gdn(pp, conv0, s0, cw, hp, nw, *, groups, t, c, row_off):
    nt = t // c
    off = row_off // c
    cc = 3 * N_HEADS128 * HEAD128
    vw = N_HEADS128 * HEAD128
    return pl.pallas_call(
        functools.partial(_gdn_body, c=c),
        out_shape=(jax.ShapeDtypeStruct((groups * t, vw), F32),
                   jax.ShapeDtypeStruct((groups, N_HEADS128, HEAD128, HEAD128), F32)),
        grid=(groups, nt),
        in_specs=[pl.BlockSpec((c, cc), lambda g, i: (off + g * nt + i, 0)),
                  pl.BlockSpec((c, vw), lambda g, i: (off + g * nt + i, cc // vw)),
                  pl.BlockSpec((c, LANES), lambda g, i: (off + g * nt + i, (cc + vw + XA_WIDTH) // LANES)),
                  pl.BlockSpec((None, 8, cc), lambda g, i: (g, 0, 0)),
                  pl.BlockSpec((None, N_HEADS128, HEAD128, HEAD128), lambda g, i: (g, 0, 0, 0)),
                  pl.BlockSpec((4, cc), lambda g, i: (0, 0)),
                  pl.BlockSpec((2, LANES), lambda g, i: (0, 0)),
                  pl.BlockSpec((1, HEAD128), lambda g, i: (0, 0))],
        out_specs=(pl.BlockSpec((c, vw), lambda g, i: (g * nt + i, 0)),
                   pl.BlockSpec((None, N_HEADS128, HEAD128, HEAD128), lambda g, i: (g, 0, 0, 0))),
        scratch_shapes=[pltpu.VMEM((c + 8, cc), F32), pltpu.VMEM((c, cc), F32),
                        pltpu.VMEM((N_HEADS128, HEAD128, HEAD128), F32)],
        compiler_params=_cparams(("parallel", "arbitrary")),
        name="gdn",
    )(pp, pp, pp, conv0, s0, cw, hp, nw)


def _hgrn_body(q_ref, f_ref, i_ref, g_ref, s0_ref, lb_ref, nw_ref, o_ref, sout_ref, st_scr, *, r, c):
    j = pl.program_id(1)

    @pl.when(j == 0)
    def _():
        for hd in range(N_HEADS128):
            st_scr[hd] = s0_ref[hd].T

    nw = nw_ref[...]
    cum_m = _b01(_tri(c, "incl"))
    nsub = c // HG_SUB
    tri8 = _iota((HG_SUB, 1), 0)
    heads = range(N_HEADS128)
    lo = [hd * HEAD128 for hd in heads]
    lb = [lb_ref[:, l:l + HEAD128] for l in lo]
    log_lb = [jnp.log(x_) for x_ in lb]
    log_1m = [jnp.log1p(-x_) for x_ in lb]
    qe_all, upd_all, ebl_all, intra_all = [], [], [], []
    for ch in range(r // c):
        r0 = ch * c
        ff = [f_ref[r0:r0 + c, l:l + HEAD128] for l in lo]
        logf = [_logaddexp(log_lb[hd], log_1m[hd] + _log_sigmoid(ff[hd])) for hd in heads]
        k = [(1.0 - lb[hd]) * _sigmoid(-ff[hd]) for hd in heads]
        q = [_silu(q_ref[r0:r0 + c, l:l + HEAD128]) * (HEAD128 ** -0.5) for l in lo]
        v = [i_ref[r0:r0 + c, l:l + HEAD128] for l in lo]
        bc = [_sel_l(cum_m, x_) for x_ in logf]
        bl = [bc[hd][c - 1:c, :] for hd in heads]
        qe_all.append([q[hd] * jnp.exp(bc[hd]) for hd in heads])
        ebl_all.append([jnp.exp(bl[hd]) for hd in heads])
        upd_all.append([_mm(v[hd], k[hd] * jnp.exp(bl[hd] - bc[hd]), _TN) for hd in heads])
        blocks = [[None] for hd in heads]
        for sb in range(1, nsub):
            b0 = sb * HG_SUB
            for hd in heads:
                bref = bc[hd][b0 - 1:b0, :]
                qs = q[hd][b0:b0 + HG_SUB, :] * jnp.exp(bc[hd][b0:b0 + HG_SUB, :] - bref)
                ks = k[hd][0:b0, :] * jnp.exp(bref - bc[hd][0:b0, :])
                att = _mm(qs, ks, _NT)
                blocks[hd].append(_mm(att, v[hd][0:b0, :]))
        bc2 = [bc[hd] * LOG2E for hd in heads]
        for sb in range(nsub):
            b0 = sb * HG_SUB
            for hd in heads:
                qb, bcb = q[hd][b0:b0 + HG_SUB, :], bc2[hd][b0:b0 + HG_SUB, :]
                ob = blocks[hd][sb]
                for s_ in range(HG_SUB):
                    w = jnp.exp2(bcb - bc2[hd][b0 + s_:b0 + s_ + 1, :])
                    col = jnp.sum(qb * k[hd][b0 + s_:b0 + s_ + 1, :] * w, axis=-1, keepdims=True)
                    t_ = jnp.where(tri8 >= s_, col, 0.0) * v[hd][b0 + s_:b0 + s_ + 1, :]
                    ob = t_ if ob is None else ob + t_
                blocks[hd][sb] = ob
        intra_all.append([blocks[hd][0] if nsub == 1 else jnp.concatenate(blocks[hd], axis=0) for hd in heads])
    st = [st_scr[hd] for hd in heads]
    for ch in range(r // c):
        r0 = ch * c
        o = [intra_all[ch][hd] + _mm(qe_all[ch][hd], st[hd], _NT) for hd in heads]
        st = [st[hd] * ebl_all[ch][hd] + upd_all[ch][hd] for hd in heads]
        for hd in heads:
            o_ref[r0:r0 + c, lo[hd]:lo[hd] + HEAD128] = _rms_rows(o[hd], nw) * _silu(g_ref[r0:r0 + c, lo[hd]:lo[hd] + HEAD128])
    for hd in heads:
        st_scr[hd] = st[hd]

    @pl.when(j == pl.num_programs(1) - 1)
    def _():
        for hd in range(N_HEADS128):
            sout_ref[hd] = st_scr[hd].T


def _hgrn(pp, s0, lb, nw, *, groups, t, r, c, row_off):
    nt = t // r
    off = row_off // r
    w = N_HEADS128 * HEAD128
    rows = lambda col: pl.BlockSpec((r, w), lambda g, i: (off + g * nt + i, col))
    return pl.pallas_call(
        functools.partial(_hgrn_body, r=r, c=c),
        out_shape=(jax.ShapeDtypeStruct((groups * t, w), F32),
                   jax.ShapeDtypeStruct((groups, N_HEADS128, HEAD128, HEAD128), F32)),
        grid=(groups, nt),
        in_specs=[rows(0), rows(1), rows(2), rows(3),
                  pl.BlockSpec((None, N_HEADS128, HEAD128, HEAD128), lambda g, i: (g, 0, 0, 0)),
                  pl.BlockSpec((1, w), lambda g, i: (0, 0)),
                  pl.BlockSpec((1, HEAD128), lambda g, i: (0, 0))],
        out_specs=(pl.BlockSpec((r, w), lambda g, i: (g * nt + i, 0)),
                   pl.BlockSpec((None, N_HEADS128, HEAD128, HEAD128), lambda g, i: (g, 0, 0, 0))),
        scratch_shapes=[pltpu.VMEM((N_HEADS128, HEAD128, HEAD128), F32)],
        compiler_params=_cparams(("parallel", "arbitrary")),
        name="hgrn2",
    )(pp, pp, pp, pp, s0, lb, nw)


def _fox_prep_body(q_ref, k_ref, fl_ref, qn_ref, kn_ref, fb_ref, bd_ref, qo_ref, ko_ref, lf_ref, fc_ref, carry):
    j = pl.program_id(1)

    @pl.when(j == 0)
    def _():
        carry[...] = jnp.zeros_like(carry)

    bd = bd_ref[...]
    qo_ref[...] = _group_rms(q_ref[...], qn_ref[...], bd, FOX_DIM)
    ko_ref[...] = _group_rms(k_ref[...], kn_ref[...], bd, FOX_DIM)
    lf = jax.nn.log_sigmoid(fl_ref[...] + fb_ref[...])
    lf_ref[...] = lf
    tr = lf.shape[0]
    fc = _sel_l(_b01(_tri(tr, "incl")), lf) + carry[...]
    fc_ref[...] = fc
    carry[...] = fc[tr - 1:tr, :]


def _fox_prep(pp, qn_t, kn_t, fb, bd64, *, groups, t, tr, row_off):
    nt = t // tr
    off = row_off // tr
    w = FOX_HEADS * FOX_DIM
    n = groups * t
    big = jax.ShapeDtypeStruct((n, w), F32)
    small = jax.ShapeDtypeStruct((n, LANES), F32)
    return pl.pallas_call(
        _fox_prep_body,
        out_shape=(big, big, small, small),
        grid=(groups, nt),
        in_specs=[pl.BlockSpec((tr, w), lambda g, i: (off + g * nt + i, 0)),
                  pl.BlockSpec((tr, w), lambda g, i: (off + g * nt + i, 1)),
                  pl.BlockSpec((tr, LANES), lambda g, i: (off + g * nt + i, (4 * w + XA_WIDTH) // LANES)),
                  pl.BlockSpec((1, w), lambda g, i: (0, 0)),
                  pl.BlockSpec((1, w), lambda g, i: (0, 0)),
                  pl.BlockSpec((1, LANES), lambda g, i: (0, 0)),
                  pl.BlockSpec((256, 256), lambda g, i: (0, 0))],
        out_specs=(pl.BlockSpec((tr, w), lambda g, i: (g * nt + i, 0)),
                   pl.BlockSpec((tr, w), lambda g, i: (g * nt + i, 0)),
                   pl.BlockSpec((tr, LANES), lambda g, i: (g * nt + i, 0)),
                   pl.BlockSpec((tr, LANES), lambda g, i: (g * nt + i, 0))),
        scratch_shapes=[pltpu.VMEM((1, LANES), F32)],
        compiler_params=_cparams(("parallel", "arbitrary")),
        name="fox_prep",
    )(pp, pp, pp, qn_t, kn_t, fb, bd64)


def _fox_prep_t_body(q_ref, k_ref, v_ref, fl_ref, qn_ref, kn_ref, fb_ref, bd_ref,
                     qo_ref, kt_ref, vt_ref, lft_ref, fc_ref, fct_ref, carry):
    j = pl.program_id(1)

    @pl.when(j == 0)
    def _():
        carry[...] = jnp.zeros_like(carry)

    bd = bd_ref[...]
    qo_ref[...] = _group_rms(q_ref[...], qn_ref[...], bd, FOX_DIM)
    kt_ref[...] = _group_rms(k_ref[...], kn_ref[...], bd, FOX_DIM).T
    vt_ref[...] = v_ref[...].T
    lf = jax.nn.log_sigmoid(fl_ref[...] + fb_ref[...])
    tr = lf.shape[0]
    fc = _sel_l(_b01(_tri(tr, "incl")), lf) + carry[...]
    fc_ref[...] = fc
    carry[...] = fc[tr - 1:tr, :]
    lft_ref[...] = lf.T[0:FOX_HEADS, :]
    fct_ref[...] = fc.T[0:FOX_HEADS, :]


def _fox_prep_t(pp, qn_t, kn_t, fb, bd64, *, groups, t, tr):
    nt = t // tr
    w = FOX_HEADS * FOX_DIM
    n = groups * t
    rows = lambda col: pl.BlockSpec((tr, w), lambda g, i: (g * nt + i, col))
    const = lambda g, i: (0, 0)
    tmaj = jax.ShapeDtypeStruct((groups, w, t), F32)
    hmaj = jax.ShapeDtypeStruct((groups, FOX_HEADS, t), F32)
    return pl.pallas_call(
        _fox_prep_t_body,
        out_shape=(jax.ShapeDtypeStruct((n, w), F32), tmaj, tmaj, hmaj, jax.ShapeDtypeStruct((n, LANES), F32), hmaj),
        grid=(groups, nt),
        in_specs=[rows(0), rows(1), rows(2),
                  pl.BlockSpec((tr, LANES), lambda g, i: (g * nt + i, (4 * w + XA_WIDTH) // LANES)),
                  pl.BlockSpec((1, w), const), pl.BlockSpec((1, w), const), pl.BlockSpec((1, LANES), const),
                  pl.BlockSpec((256, 256), const)],
        out_specs=(pl.BlockSpec((tr, w), lambda g, i: (g * nt + i, 0)),
                   pl.BlockSpec((None, w, tr), lambda g, i: (g, 0, i)),
                   pl.BlockSpec((None, w, tr), lambda g, i: (g, 0, i)),
                   pl.BlockSpec((None, FOX_HEADS, tr), lambda g, i: (g, 0, i)),
                   pl.BlockSpec((tr, LANES), lambda g, i: (g * nt + i, 0)),
                   pl.BlockSpec((None, FOX_HEADS, tr), lambda g, i: (g, 0, i))),
        scratch_shapes=[pltpu.VMEM((1, LANES), F32)],
        compiler_params=_cparams(("parallel", "arbitrary")),
        name="fox_prep_t",
    )(pp, pp, pp, pp, qn_t, kn_t, fb, bd64)


def _fox_flash_body(q_ref, k_ref, v_ref, og_ref, fq_ref, fk_ref, o_ref, m_scr, l_scr, acc_scr, *, tq):
    hp = pl.program_id(1)
    qi = pl.program_id(2)
    lane = _iota((1, LANES), 1)
    q = q_ref[...] * (FOX_DIM ** -0.5)
    fq = fq_ref[...]
    qs, fqs = [], []
    for j in range(2):
        qs.append(jnp.where((lane // FOX_DIM) == j, q, 0.0).astype(BF16))
        fqs.append(_sel_r(fq, _b01(_iota((LANES, LANES), 0) == 2 * hp + j)))
        m_scr[j] = jnp.full((tq, LANES), -jnp.inf, F32)
        l_scr[j] = jnp.zeros((tq, LANES), F32)
        acc_scr[j] = jnp.zeros((tq, LANES), F32)

    def rep(x, n):
        return x if n == LANES else jnp.concatenate([x] * (n // LANES), axis=1)

    def block(ki, masked):
        k0 = pl.multiple_of(ki * tq, tq)
        kb = k_ref[:, pl.ds(k0, tq)].astype(BF16)
        vb = v_ref[pl.ds(k0, tq), :].astype(BF16)
        fk = [fk_ref[pl.ds(2 * hp + j, 1), pl.ds(k0, tq)] for j in range(2)]
        chains = [(j, r0) for r0 in range(0, tq, FOX_RG) for j in range(2)]
        nk = [r0 + FOX_RG if masked else tq for (_, r0) in chains]
        s = [jnp.dot(qs[j][r0:r0 + FOX_RG, :], kb[:, :n_], preferred_element_type=F32)
             + rep(fqs[j][r0:r0 + FOX_RG, :], n_) - fk[j][:, :n_] for (j, r0), n_ in zip(chains, nk)]
        if masked:
            s = [jnp.where(_iota((FOX_RG, n_), 1) <= _iota((FOX_RG, n_), 0) + r0, s_, -jnp.inf)
                 for (_, r0), n_, s_ in zip(chains, nk, s)]
        m_old = [m_scr[j, r0:r0 + FOX_RG, :] for (j, r0) in chains]
        m_new = [jnp.maximum(mo_, jnp.max(s_, axis=-1, keepdims=True)) for mo_, s_ in zip(m_old, s)]
        alpha = [jnp.exp(mo_ - mn_) for mo_, mn_ in zip(m_old, m_new)]
        p = [jnp.exp(s_ - rep(mn_, n_)) for s_, mn_, n_ in zip(s, m_new, nk)]
        pv = [jnp.dot(p_.astype(BF16), vb[:n_, :], preferred_element_type=F32) for p_, n_ in zip(p, nk)]
        for c_, (j, r0) in enumerate(chains):
            rows = slice(r0, r0 + FOX_RG)
            l_scr[j, rows, :] = alpha[c_] * l_scr[j, rows, :] + jnp.sum(p[c_], axis=-1, keepdims=True)
            acc_scr[j, rows, :] = alpha[c_] * acc_scr[j, rows, :] + pv[c_]
            m_scr[j, rows, :] = m_new[c_]

    def body(ki, carry):
        block(ki, False)
        return carry

    lax.fori_loop(0, qi, body, 0)
    block(qi, True)
    o0 = acc_scr[0] / l_scr[0]
    o1 = acc_scr[1] / l_scr[1]
    o_ref[...] = jnp.where(lane < FOX_DIM, o0, o1) * _sigmoid(og_ref[...])


def _fox_flash(qn, kt, pp, fc, fct, *, groups, t, tq):
    nq = t // tq
    w = FOX_HEADS * FOX_DIM
    hp_n = w // LANES
    vblk = 2 * w // LANES
    gblk = 3 * w // LANES
    return pl.pallas_call(
        functools.partial(_fox_flash_body, tq=tq),
        out_shape=jax.ShapeDtypeStruct((groups * t, w), F32),
        grid=(groups, hp_n, nq),
        in_specs=[pl.BlockSpec((tq, LANES), lambda g, h, i: (g * nq + i, h)),
                  pl.BlockSpec((None, LANES, t), lambda g, h, i: (g, h, 0)),
                  pl.BlockSpec((t, LANES), lambda g, h, i: (g, vblk + h)),
                  pl.BlockSpec((tq, LANES), lambda g, h, i: (g * nq + i, gblk + h)),
                  pl.BlockSpec((tq, LANES), lambda g, h, i: (g * nq + i, 0)),
                  pl.BlockSpec((None, FOX_HEADS, t), lambda g, h, i: (g, 0, 0))],
        out_specs=pl.BlockSpec((tq, LANES), lambda g, h, i: (g * nq + i, h)),
        scratch_shapes=[pltpu.VMEM((2, tq, LANES), F32)] * 3,
        compiler_params=_cparams(("parallel", "parallel", "arbitrary")),
        name="fox_flash",
    )(qn, kt, pp, pp, fc, fct)


def _fox_sample_body(pt_ref, *refs, gp, tnew):
    del pt_ref
    k_refs = refs[0:gp]
    v_refs = refs[gp:2 * gp]
    lf_refs = refs[2 * gp:3 * gp]
    q_ref, kn_ref, vn_ref, og_ref, lfn_ref, o_ref, q_scr, m_scr, l_scr, acc, carry = refs[3 * gp:]
    pg = pl.program_id(1)

    @pl.when(pg == 0)
    def _():
        qs = q_ref[...] * (FOX_DIM ** -0.5)
        for hd in range(FOX_HEADS):
            q_scr[hd] = qs[:, hd * FOX_DIM:(hd + 1) * FOX_DIM]
        m_scr[...] = jnp.full(m_scr.shape, -jnp.inf, F32)
        l_scr[...] = jnp.zeros_like(l_scr)
        acc[...] = jnp.zeros_like(acc)
        carry[...] = jnp.zeros_like(carry)

    fnew = _sel_l(_b01(_tri(tnew, "incl")), lfn_ref[...])

    heads = range(FOX_HEADS)

    def update(s, pv_fn):
        m_old = [m_scr[hd] for hd in heads]
        m_new = [jnp.maximum(m_old[hd], jnp.max(s[hd], axis=-1, keepdims=True)) for hd in heads]
        alpha = [jnp.exp(m_old[hd] - m_new[hd]) for hd in heads]
        p = [jnp.exp(s[hd] - m_new[hd]) for hd in heads]
        pv = [pv_fn(hd, p[hd].astype(BF16)) for hd in heads]
        for hd in heads:
            l_scr[hd] = alpha[hd] * l_scr[hd] + jnp.sum(p[hd], axis=-1, keepdims=True)
            acc[hd] = alpha[hd] * acc[hd] + pv[hd]
            m_scr[hd] = m_new[hd]

    later = _b01(_tri(PAGE, "strict"))
    rests = []
    run = carry[...]
    for j in range(gp):
        lft = lf_refs[j][...]
        rests.append(_sel_r(lft, later) + run)
        run = run + jnp.sum(lft, axis=-1, keepdims=True)
    carry[...] = run

    qh = [q_scr[hd].astype(BF16) for hd in heads]
    s = [jnp.concatenate(
        [jnp.dot(qh[hd], k_refs[j][hd].astype(BF16), preferred_element_type=F32) + rests[j][hd:hd + 1, :]
         for j in range(gp)], axis=1) + fnew[:, hd:hd + 1] for hd in heads]

    def pv_past(hd, p):
        parts = [lax.dot_general(p[:, j * PAGE:(j + 1) * PAGE], v_refs[j][hd].astype(BF16), _NT,
                                 preferred_element_type=F32) for j in range(gp)]
        while len(parts) > 1:
            parts = [parts[i] + parts[i + 1] for i in range(0, len(parts), 2)]
        return parts[0]

    update(s, pv_past)

    @pl.when(pg == pl.num_programs(1) - 1)
    def _():
        fnew_t = fnew.T
        causal = _tri(tnew, "incl")
        lo = [hd * FOX_DIM for hd in heads]
        qh = [q_scr[hd].astype(BF16) for hd in heads]
        s = [lax.dot_general(qh[hd], kn_ref[:, lo[hd]:lo[hd] + FOX_DIM].astype(BF16), _NT, preferred_element_type=F32)
             + fnew[:, hd:hd + 1] - fnew_t[hd:hd + 1, :] for hd in heads]
        s = [jnp.where(causal, s_, -jnp.inf) for s_ in s]
        update(s, lambda hd, p: jnp.dot(p, vn_ref[:, lo[hd]:lo[hd] + FOX_DIM].astype(BF16), preferred_element_type=F32))
        o_ref[...] = jnp.concatenate([acc[hd] / l_scr[hd] for hd in heads], axis=1) * _sigmoid(og_ref[...])


def _fox_sample(page_table, ckt, cvt, clft, qn, kn, pp, lfn, *, tnew, row_off, gp):
    ns, npages = page_table.shape
    w = FOX_HEADS * FOX_DIM
    ngrp = npages // gp
    off = row_off // tnew

    def page_map(j, nd):
        return lambda s, g, pt: (pt[s, npages - 1 - (g * gp + j)],) + (0,) * nd

    kv_specs = [pl.BlockSpec((None, FOX_HEADS, FOX_DIM, PAGE), page_map(j, 3)) for j in range(gp)]
    lf_specs = [pl.BlockSpec((None, FOX_HEADS, PAGE), page_map(j, 2)) for j in range(gp)]
    row = lambda s, g, pt: (s, 0)
    grid_spec = pltpu.PrefetchScalarGridSpec(
        num_scalar_prefetch=1,
        grid=(ns, ngrp),
        in_specs=kv_specs + kv_specs + lf_specs + [
            pl.BlockSpec((tnew, w), row),
            pl.BlockSpec((tnew, w), row),
            pl.BlockSpec((tnew, w), lambda s, g, pt: (off + s, 2)),
            pl.BlockSpec((tnew, w), lambda s, g, pt: (off + s, 3)),
            pl.BlockSpec((tnew, LANES), row)],
        out_specs=pl.BlockSpec((tnew, w), row),
        scratch_shapes=[pltpu.VMEM((FOX_HEADS, tnew, FOX_DIM), F32), pltpu.VMEM((FOX_HEADS, tnew, 1), F32),
                        pltpu.VMEM((FOX_HEADS, tnew, 1), F32), pltpu.VMEM((FOX_HEADS, tnew, FOX_DIM), F32),
                        pltpu.VMEM((FOX_HEADS, 1), F32)],
    )
    return pl.pallas_call(
        functools.partial(_fox_sample_body, gp=gp, tnew=tnew),
        out_shape=jax.ShapeDtypeStruct((ns * tnew, w), F32),
        grid_spec=grid_spec,
        compiler_params=_cparams(("parallel", "arbitrary")),
        name="fox_sample",
    )(page_table, *([ckt] * gp), *([cvt] * gp), *([clft] * gp), qn, kn, pp, pp, lfn)


def _cmlp_body(u_ref, v_ref, lg_ref, lbias_ref, ws_ref, bsb_ref, o_ref, vo_ref):
    u = jax.nn.gelu(u_ref[...], approximate=True)
    z = jax.nn.gelu(v_ref[...], approximate=True)
    mu = jnp.mean(z, axis=-1, keepdims=True)
    zc = z - mu
    v = zc * lax.rsqrt(jnp.mean(zc * zc, axis=-1, keepdims=True) + EPS) * lg_ref[...] + lbias_ref[...]
    vo_ref[...] = v
    tril = _tri(ws_ref.shape[1], "incl")
    gd = v.shape[1] // CM_GROUPS
    for g in range(CM_GROUPS):
        wm = jnp.where(tril, ws_ref[g], 0.0)
        mixed = _mm(wm, v[:, g * gd:(g + 1) * gd]) + bsb_ref[g]
        o_ref[:, g * gd:(g + 1) * gd] = u[:, g * gd:(g + 1) * gd] * mixed


def _chunk_mlp(pp, ln_g, ln_b, ws, bsb, *, nrows, row_off):
    w = ln_g.shape[1]
    cr = ws.shape[1]
    off = row_off // cr
    out = jax.ShapeDtypeStruct((nrows, w), F32)
    const2 = lambda i: (0, 0)
    const3 = lambda i: (0, 0, 0)
    return pl.pallas_call(
        _cmlp_body,
        out_shape=(out, out),
        grid=(nrows // cr,),
        in_specs=[pl.BlockSpec((cr, w), lambda i: (off + i, 0)),
                  pl.BlockSpec((cr, w), lambda i: (off + i, 1)),
                  pl.BlockSpec((1, w), const2), pl.BlockSpec((1, w), const2),
                  pl.BlockSpec((CM_GROUPS, cr, cr), const3),
                  pl.BlockSpec((CM_GROUPS, cr, w // CM_GROUPS), const3)],
        out_specs=(pl.BlockSpec((cr, w), lambda i: (i, 0)),
                   pl.BlockSpec((cr, w), lambda i: (i, 0))),
        compiler_params=_cparams(("parallel",)),
        name="chunk_mlp",
    )(pp, pp, ln_g, ln_b, ws, bsb)


def _pad_cols(w, mult):
    pad = (-w.shape[1]) % mult
    return jnp.pad(w, ((0, 0), (0, pad))) if pad else w


def _lane_row(v, start=0):
    return jnp.zeros((1, LANES), F32).at[0, start:start + v.shape[0]].set(v.astype(F32))


def kernel(x_prompt, x_sample, mem_prompt, state_a_conv, state_a_ssm, state_b_ssm, cache_c_k, cache_c_v, cache_c_logf, cache_mem_k, cache_mem_v, page_table, norm_mix, w_out, norm_mlp, w_up, w_down, mem_norm, w_mem_kv, xa_qnorm, xa_knorm, w_in_a, a_conv_w, a_log, a_dt_bias, a_norm_w, w_in_b, hg_lb, b_norm_w, w_in_c, c_fbias, c_qnorm, c_knorm, w_in_d, d_ln_g, d_ln_b, d_ws, d_bs):
    bp, seq, d = x_prompt.shape
    ds, dseq, _ = x_sample.shape
    depth = norm_mix.shape[0]
    np_rows, ns_rows = bp * seq, ds * dseq
    tm_p = 512 if np_rows % 512 == 0 else 128
    tm_s = 256 if ns_rows % 256 == 0 else 8
    w1024 = N_HEADS128 * HEAD128
    cc = 3 * w1024

    h_p, h_s = x_prompt.reshape(np_rows, d), x_sample.reshape(ns_rows, d)
    bd64 = (jnp.arange(256)[:, None] // 64 == jnp.arange(256)[None, :] // 64).astype(BF16)

    mk_all, mv_all = _memory_kv(mem_prompt, mem_norm[:, None, :], jnp.swapaxes(w_mem_kv, 1, 2).astype(BF16),
                                xa_knorm[:, :, None])

    lb_w = jax.nn.softmax(hg_lb.astype(F32), axis=0)
    lower_bounds = jnp.cumsum(lb_w, axis=0) - lb_w[0]

    outs = {}
    for l in range(depth):
        kind, j = l % 4, l // 4
        if kind == 0:
            wi = w_in_a[j]
            w_packed = jnp.concatenate([wi[:, :cc + w1024], wi[:, cc + w1024 + 16:], _pad_cols(wi[:, cc + w1024:cc + w1024 + 16], LANES)], axis=1)
            xq_blk = (cc + w1024) // XA_WIDTH
        elif kind == 1:
            w_packed = w_in_b[j]
            xq_blk = 4 * w1024 // XA_WIDTH
        elif kind == 2:
            wi = w_in_c[j]
            w_packed = jnp.concatenate([wi[:, :4 * w1024], wi[:, 4 * w1024 + 16:], _pad_cols(wi[:, 4 * w1024:4 * w1024 + 16], LANES)], axis=1)
            xq_blk = 4 * w1024 // XA_WIDTH
        else:
            w_packed = w_in_d[j]
            xq_blk = 2 * w1024 // XA_WIDTH
        w_bf = w_packed.astype(BF16)
        g_mix = norm_mix[l][None, :]
        pp_p = _norm_proj(h_p, g_mix, w_bf, tm=tm_p)
        pp_s = _norm_proj(h_s, g_mix, w_bf, tm=tm_s)

        qn_t = jnp.tile(xa_qnorm[l], XA_HEADS)[None, :]
        xo_p = _mem_attend(pp_p, mk_all[l], mv_all[l], qn_t, bd64, groups=bp, t=seq, tt=512, row_off=0, col_blk=xq_blk)
        to_hdn = lambda a: jnp.transpose(a, (0, 2, 3, 1)).reshape(ds, XA_WIDTH, N_MEM)
        xo_s = _mem_attend(pp_s, to_hdn(cache_mem_k[l]), to_hdn(cache_mem_v[l]),
                           qn_t, bd64, groups=ds, t=dseq, tt=dseq, row_off=0, col_blk=xq_blk)

        if kind == 0:
            hp = jnp.concatenate([_lane_row(a_log[j], 8), _lane_row(a_dt_bias[j], 8)], axis=0)
            nw = a_norm_w[j][None, :]
            conv0_p = jnp.zeros((bp, 8, cc), F32)
            conv0_s = jnp.pad(state_a_conv[j], ((0, 0), (5, 0), (0, 0)))
            mo_p, st_p = _gdn(pp_p, conv0_p, jnp.zeros((bp, N_HEADS128, HEAD128, HEAD128), F32), a_conv_w[j], hp, nw,
                              groups=bp, t=seq, c=GDN_CHUNK, row_off=0)
            mo_s, st_s = _gdn(pp_s, conv0_s, state_a_ssm[j], a_conv_w[j], hp, nw,
                              groups=ds, t=dseq, c=dseq, row_off=0)
            outs.setdefault("a_conv_p", []).append(pp_p[:, :cc].reshape(bp, seq, cc)[:, seq - 3:])
            outs.setdefault("a_conv_s", []).append(pp_s[:, :cc].reshape(ds, dseq, cc)[:, dseq - 3:])
            outs.setdefault("a_ssm_p", []).append(st_p)
            outs.setdefault("a_ssm_s", []).append(st_s)
        elif kind == 1:
            lb = lower_bounds[l][None, :]
            nw = b_norm_w[j][None, :]
            mo_p, st_p = _hgrn(pp_p, jnp.zeros((bp, N_HEADS128, HEAD128, HEAD128), F32), lb, nw,
                               groups=bp, t=seq, r=128, c=HG_CHUNK, row_off=0)
            mo_s, st_s = _hgrn(pp_s, state_b_ssm[j], lb, nw, groups=ds, t=dseq, r=dseq, c=dseq, row_off=0)
            outs.setdefault("b_ssm_p", []).append(st_p)
            outs.setdefault("b_ssm_s", []).append(st_s)
        elif kind == 2:
            qn_f = jnp.tile(c_qnorm[j], FOX_HEADS)[None, :]
            kn_f = jnp.tile(c_knorm[j], FOX_HEADS)[None, :]
            fb = _lane_row(c_fbias[j])
            q_p, kt_p, vt_p, lft_p, fc_p, fct_p = _fox_prep_t(pp_p, qn_f, kn_f, fb, bd64, groups=bp, t=seq, tr=256)
            q_s, k_s, lf_s, _ = _fox_prep(pp_s, qn_f, kn_f, fb, bd64, groups=ds, t=dseq, tr=dseq, row_off=0)
            mo_p = _fox_flash(q_p, kt_p, pp_p, fc_p, fct_p, groups=bp, t=seq, tq=FOX_TQ)
            pos_minor = lambda a: jnp.transpose(a, (0, 2, 3, 1))
            mo_s = _fox_sample(page_table, pos_minor(cache_c_k[j]), pos_minor(cache_c_v[j]),
                               jnp.swapaxes(cache_c_logf[j], 1, 2), q_s, k_s, pp_s, lf_s,
                               tnew=dseq, row_off=0, gp=PAGES_PER_STEP)
            time_major = lambda a: jnp.transpose(a.reshape(bp, FOX_HEADS, FOX_DIM, seq), (0, 3, 1, 2))
            outs.setdefault("c_k_p", []).append(time_major(kt_p))
            outs.setdefault("c_v_p", []).append(time_major(vt_p))
            outs.setdefault("c_lf_p", []).append(jnp.swapaxes(lft_p, 1, 2))
            outs.setdefault("c_k_s", []).append(k_s.reshape(ds, dseq, FOX_HEADS, FOX_DIM))
            outs.setdefault("c_v_s", []).append(pp_s[:, 2 * w1024:3 * w1024].reshape(ds, dseq, FOX_HEADS, FOX_DIM))
            outs.setdefault("c_lf_s", []).append(lf_s[:, :FOX_HEADS].reshape(ds, dseq, FOX_HEADS))
        else:
            gd = w1024 // CM_GROUPS
            lg, lbias = d_ln_g[j][None, :], d_ln_b[j][None, :]
            mo_p, _ = _chunk_mlp(pp_p, lg, lbias, d_ws[j], jnp.broadcast_to(d_bs[j][:, :, None], (CM_GROUPS, CM_CHUNK, gd)),
                                 nrows=np_rows, row_off=0)
            mo_s, v_s = _chunk_mlp(pp_s, lg, lbias, d_ws[j][:, :dseq, :dseq],
                                   jnp.broadcast_to(d_bs[j][:, :dseq, None], (CM_GROUPS, dseq, gd)), nrows=ns_rows, row_off=0)
            outs.setdefault("d_v_s", []).append(v_s.reshape(ds, dseq, w1024))

        wo = w_out[l].astype(BF16)
        mlp_w = (wo[:w1024], wo[w1024:], norm_mlp[l][None, :], w_up[l].astype(BF16), w_down[l].astype(BF16))
        h_p = _out_mlp(h_p, mo_p, xo_p, *mlp_w, tm=tm_p)
        h_s = _out_mlp(h_s, mo_s, xo_s, *mlp_w, tm=tm_s)

    st = lambda name: jnp.stack(outs[name])
    mem_out = lambda a: jnp.transpose(a.reshape(depth, bp, XA_HEADS, XA_DIM, N_MEM), (0, 1, 4, 2, 3))
    return (h_p.reshape(bp, seq, d), h_s.reshape(ds, dseq, d),
            st("a_conv_p"), st("a_conv_s"), st("a_ssm_p"), st("a_ssm_s"),
            st("b_ssm_p"), st("b_ssm_s"),
            st("c_k_p"), st("c_v_p"), st("c_lf_p"), st("c_k_s"), st("c_v_s"), st("c_lf_s"),
            st("d_v_s"), mem_out(mk_all), mem_out(mv_all))
```

```python
import functools
import math

import jax
import jax.numpy as jnp
from jax import lax
from jax.experimental import pallas as pl
from jax.experimental.pallas import tpu as pltpu

F32 = jnp.float32
BF16 = jnp.bfloat16
EPS = 1e-6
LOG2E = 1.4426950408889634

LANES = 128
SUBLANES = 8
VMEM_LIMIT = 56 * 1024 * 1024

XA_HEADS, XA_DIM, N_MEM = 4, 64, 256
XA_WIDTH = XA_HEADS * XA_DIM
HEAD128 = 128
N_HEADS128 = 8
FOX_DIM, FOX_HEADS = 64, 16
PAGE = 128
CM_CHUNK, CM_GROUPS = 128, 8
GDN_CHUNK, HG_CHUNK, HG_SUB = 64, 32, 8
GDN_ROWS = 128
FOX_TQ = 512
FOX_RG = 128
FOX_BIAS_ROWS = 16
PAGES_PER_STEP = 16


def _cparams(sem):
    return pltpu.CompilerParams(dimension_semantics=sem, vmem_limit_bytes=VMEM_LIMIT)


def _iota(shape, dim):
    return lax.broadcasted_iota(jnp.int32, shape, dim)


def _split3(x):
    hi = x.astype(BF16)
    r = x - hi.astype(F32)
    mid = r.astype(BF16)
    lo = (r - mid.astype(F32)).astype(BF16)
    return hi, mid, lo


def _split2(x):
    hi = x.astype(BF16)
    lo = (x - hi.astype(F32)).astype(BF16)
    return hi, lo


_NN = (((1,), (0,)), ((), ()))
_NT = (((1,), (1,)), ((), ()))
_TN = (((0,), (0,)), ((), ()))


def _mm(a, b, dims=_NN):
    return lax.dot_general(a.astype(BF16), b.astype(BF16), dims, preferred_element_type=F32)


def _sel_l(m, x, dims=_NN):
    out = None
    for p in _split3(x):
        t = lax.dot_general(m, p, dims, preferred_element_type=F32)
        out = t if out is None else out + t
    return out


def _sel_r(x, m, dims=_NN):
    out = None
    for p in _split3(x):
        t = lax.dot_general(p, m, dims, preferred_element_type=F32)
        out = t if out is None else out + t
    return out


def _mm3(a, b):
    ah, al = _split2(a)
    bh, bl = _split2(b)
    d = functools.partial(jnp.dot, preferred_element_type=F32)
    return d(ah, bh) + d(ah, bl) + d(al, bh)


def _sigmoid(x):
    return jax.nn.sigmoid(x)


def _silu(x):
    return x * jax.nn.sigmoid(x)


def _log_sigmoid(x):
    return jnp.minimum(x, 0.0) - jnp.log1p(jnp.exp(-jnp.abs(x)))


def _logaddexp(a, b):
    return jnp.maximum(a, b) + jnp.log1p(jnp.exp(-jnp.abs(a - b)))


def _rms_rows(x, g):
    return x * lax.rsqrt(jnp.mean(x * x, axis=-1, keepdims=True) + EPS) * g


def _group_rms(x, gain, bd, gsz):
    w = x.shape[1]
    x2 = x * x
    parts = []
    for c in range(0, w, 256):
        parts.append(_sel_r(x2[:, c:c + 256], bd))
    ms = (parts[0] if len(parts) == 1 else jnp.concatenate(parts, axis=1)) * (1.0 / gsz)
    return x * lax.rsqrt(ms + EPS) * gain


def _tri(n, kind):
    r, c = _iota((n, n), 0), _iota((n, n), 1)
    if kind == "incl":
        return r >= c
    if kind == "strict":
        return r > c
    if kind == "upper_incl":
        return r <= c
    if kind == "upper_strict":
        return r < c
    raise ValueError(kind)


def _b01(mask):
    return jnp.where(mask, 1.0, 0.0).astype(BF16)


def _proj_body(x_ref, g_ref, w_ref, o_ref, *, tn):
    xn = _rms_rows(x_ref[...], g_ref[...]).astype(BF16)
    wp = w_ref.shape[1]
    for c in range(0, wp, tn):
        e = min(c + tn, wp)
        o_ref[:, c:e] = jnp.dot(xn, w_ref[:, c:e], preferred_element_type=F32)


def _norm_proj(h, g, w, *, tm, tn=512):
    n, d = h.shape
    wp = w.shape[1]
    return pl.pallas_call(
        functools.partial(_proj_body, tn=tn),
        out_shape=jax.ShapeDtypeStruct((n, wp), F32),
        grid=(n // tm,),
        in_specs=[pl.BlockSpec((tm, d), lambda i: (i, 0)),
                  pl.BlockSpec((1, d), lambda i: (0, 0)),
                  pl.BlockSpec((d, wp), lambda i: (0, 0), pipeline_mode=pl.Buffered(1))],
        out_specs=pl.BlockSpec((tm, wp), lambda i: (i, 0)),
        compiler_params=_cparams(("parallel",)),
        name="norm_proj",
    )(h, g, w)


def _out_mlp_body(h_ref, mo_ref, xo_ref, wo1_ref, wo2_ref, g_ref, wu_ref, wd_ref, o_ref, *, tf):
    h2 = (h_ref[...]
          + jnp.dot(mo_ref[...].astype(BF16), wo1_ref[...], preferred_element_type=F32)
          + jnp.dot(xo_ref[...].astype(BF16), wo2_ref[...], preferred_element_type=F32))
    xn = _rms_rows(h2, g_ref[...]).astype(BF16)
    acc = None
    dff = wu_ref.shape[1]
    for c in range(0, dff, tf):
        up = jnp.dot(xn, wu_ref[:, c:c + tf], preferred_element_type=F32)
        act = jnp.square(jnp.maximum(up, 0.0)).astype(BF16)
        t = jnp.dot(act, wd_ref[c:c + tf, :], preferred_element_type=F32)
        acc = t if acc is None else acc + t
    o_ref[...] = h2 + acc


def _out_mlp(h, mo, xo, wo1, wo2, g, wu, wd, *, tm, tf=512):
    n, d = h.shape
    dff = wu.shape[1]
    xw = xo.shape[1]
    const = lambda i: (0, 0)
    one = pl.Buffered(1)
    return pl.pallas_call(
        functools.partial(_out_mlp_body, tf=tf),
        out_shape=jax.ShapeDtypeStruct((n, d), F32),
        grid=(n // tm,),
        in_specs=[pl.BlockSpec((tm, d), lambda i: (i, 0)),
                  pl.BlockSpec((tm, d), lambda i: (i, 0)),
                  pl.BlockSpec((tm, xw), lambda i: (i, 0)),
                  pl.BlockSpec((d, d), const, pipeline_mode=one),
                  pl.BlockSpec((xw, d), const, pipeline_mode=one),
                  pl.BlockSpec((1, d), const),
                  pl.BlockSpec((d, dff), const, pipeline_mode=one),
                  pl.BlockSpec((dff, d), const, pipeline_mode=one)],
        out_specs=pl.BlockSpec((tm, d), lambda i: (i, 0)),
        compiler_params=_cparams(("parallel",)),
        name="out_mlp",
    )(h, mo, xo, wo1, wo2, g, wu, wd)


def _memkv_body(x_ref, g_ref, wt_ref, kn_ref, k_ref, v_ref):
    xn = _rms_rows(x_ref[...], g_ref[...]).astype(BF16)
    kvt = lax.dot_general(wt_ref[...], xn, _NT, preferred_element_type=F32)
    kn = kn_ref[...]
    for hd in range(XA_HEADS):
        kt = kvt[hd * XA_DIM:(hd + 1) * XA_DIM, :]
        ms = jnp.mean(kt * kt, axis=0, keepdims=True)
        k_ref[hd * XA_DIM:(hd + 1) * XA_DIM, :] = kt * lax.rsqrt(ms + EPS) * kn
    v_ref[...] = kvt[XA_WIDTH:, :]


def _memory_kv(mem, mem_norm, w_kv_t, knorm_col):
    b, nm, d = mem.shape
    nl = w_kv_t.shape[0]
    out = jax.ShapeDtypeStruct((nl, b, XA_WIDTH, nm), F32)
    return pl.pallas_call(
        _memkv_body,
        out_shape=(out, out),
        grid=(nl, b),
        in_specs=[pl.BlockSpec((None, nm, d), lambda l, i: (i, 0, 0)),
                  pl.BlockSpec((None, 1, d), lambda l, i: (l, 0, 0)),
                  pl.BlockSpec((None, 2 * XA_WIDTH, d), lambda l, i: (l, 0, 0)),
                  pl.BlockSpec((None, XA_DIM, 1), lambda l, i: (l, 0, 0))],
        out_specs=(pl.BlockSpec((None, None, XA_WIDTH, nm), lambda l, i: (l, i, 0, 0)),
                   pl.BlockSpec((None, None, XA_WIDTH, nm), lambda l, i: (l, i, 0, 0))),
        compiler_params=_cparams(("parallel", "parallel")),
        name="memory_kv",
    )(mem, mem_norm, w_kv_t, knorm_col)


def _xattn_body(q_ref, mk_ref, mv_ref, qn_ref, bd_ref, o_ref, *, gs, tt, rg):
    q = _group_rms(q_ref[...], qn_ref[...], bd_ref[...], XA_DIM) * (XA_DIM ** -0.5)
    lane = _iota((1, XA_WIDTH), 1) // XA_DIM
    row = _iota((XA_WIDTH, 1), 0) // XA_DIM
    mkt = [mk_ref[g].astype(BF16) for g in range(gs)]
    mvt = [[jnp.where(row == hd, mv_ref[g], 0.0).astype(BF16) for hd in range(XA_HEADS)] for g in range(gs)]
    chains = [(g, r0, hd) for g in range(gs) for r0 in range(0, tt, rg) for hd in range(XA_HEADS)]
    qc = [jnp.where(lane == hd, q[g * tt + r0:g * tt + r0 + rg, :], 0.0).astype(BF16) for (g, r0, hd) in chains]
    s = [jnp.dot(q_, mkt[g], preferred_element_type=F32) for q_, (g, _, _) in zip(qc, chains)]
    p = [jnp.exp(s_ - jnp.max(s_, axis=-1, keepdims=True)) for s_ in s]
    inv = [1.0 / jnp.sum(p_, axis=-1, keepdims=True) for p_ in p]
    t = [lax.dot_general(p_.astype(BF16), mvt[g][hd], _NT, preferred_element_type=F32) * i_
         for p_, i_, (g, _, hd) in zip(p, inv, chains)]
    for c0 in range(0, len(chains), XA_HEADS):
        g, r0, _ = chains[c0]
        o_ref[g * tt + r0:g * tt + r0 + rg, :] = (t[c0] + t[c0 + 1]) + (t[c0 + 2] + t[c0 + 3])


def _mem_attend(pp, mk, mv, qn_t, bd64, *, groups, t, tt, gs, rg, col_blk):
    nt = t // tt
    assert gs == 1 or nt == 1
    return pl.pallas_call(
        functools.partial(_xattn_body, gs=gs, tt=tt, rg=rg),
        out_shape=jax.ShapeDtypeStruct((groups * t, XA_WIDTH), F32),
        grid=(groups // gs, nt),
        in_specs=[pl.BlockSpec((gs * tt, XA_WIDTH), lambda g, i: (g * nt + i, col_blk)),
                  pl.BlockSpec((gs, XA_WIDTH, N_MEM), lambda g, i: (g, 0, 0)),
                  pl.BlockSpec((gs, XA_WIDTH, N_MEM), lambda g, i: (g, 0, 0)),
                  pl.BlockSpec((1, XA_WIDTH), lambda g, i: (0, 0)),
                  pl.BlockSpec((256, 256), lambda g, i: (0, 0))],
        out_specs=pl.BlockSpec((gs * tt, XA_WIDTH), lambda g, i: (g * nt + i, 0)),
        compiler_params=_cparams(("parallel", "parallel")),
        name="mem_attend",
    )(pp, mk, mv, qn_t, bd64)


def _gdn_body(qkv_ref, z_ref, ba_ref, conv0_ref, s0_ref, cw_ref, hp_ref, nw_ref,
              o_ref, sout_ref, xbuf, cbuf, s_scr, *, r, c):
    j = pl.program_id(1)
    kw = N_HEADS128 * HEAD128

    @pl.when(j == 0)
    def _():
        xbuf[0:8, :] = conv0_ref[...]
        s_scr[...] = s0_ref[...]

    x = qkv_ref[...]
    xbuf[8:8 + r, :] = x
    cw = cw_ref[...]
    conv = x * cw[3:4, :]
    for i in range(1, 4):
        conv = conv + xbuf[8 - i:8 - i + r, :] * cw[3 - i:4 - i, :]
    cbuf[...] = _silu(conv)
    xbuf[0:8, :] = xbuf[r:r + 8, :]

    hp = hp_ref[...]
    incl = _tri(c, "incl")
    strict = _tri(c, "strict")
    cum_l = _b01(incl)
    cum_u = _b01(_tri(c, "upper_incl"))
    eye = jnp.where(_iota((c, c), 0) == _iota((c, c), 1), 1.0, 0.0)
    nw = nw_ref[...]
    heads = range(N_HEADS128)
    lo = [hd * HEAD128 for hd in heads]

    pre = []
    for ch in range(r // c):
        rows = slice(ch * c, (ch + 1) * c)
        ba = ba_ref[rows, :]
        beta_all = _sigmoid(ba)
        g_all = -jnp.exp(hp[0:1, :]) * jax.nn.softplus(ba + hp[1:2, :])
        gcum = _sel_l(cum_l, g_all)
        gcum_t = _sel_r(g_all, cum_u, _TN)
        eg_all = jnp.exp(gcum)
        q = [cbuf[rows, l:l + HEAD128] for l in lo]
        k = [cbuf[rows, kw + l:kw + l + HEAD128] for l in lo]
        v = [cbuf[rows, 2 * kw + l:2 * kw + l + HEAD128] for l in lo]
        q = [x_ * lax.rsqrt(jnp.sum(x_ * x_, axis=-1, keepdims=True) + EPS) * (HEAD128 ** -0.5) for x_ in q]
        k = [x_ * lax.rsqrt(jnp.sum(x_ * x_, axis=-1, keepdims=True) + EPS) for x_ in k]
        bcol = [beta_all[:, hd:hd + 1] for hd in heads]
        gc = [gcum[:, 8 + hd:9 + hd] for hd in heads]
        egc = [eg_all[:, 8 + hd:9 + hd] for hd in heads]
        gl = [gcum[c - 1:c, 8 + hd:9 + hd] for hd in heads]
        decay = [jnp.exp(jnp.where(incl, gc[hd] - gcum_t[8 + hd:9 + hd, :], -jnp.inf)) for hd in heads]
        kk = [_mm(k[hd], k[hd], _NT) for hd in heads]
        qk = [_mm(q[hd], k[hd], _NT) for hd in heads]
        a = [jnp.where(strict, bcol[hd] * kk[hd] * decay[hd], 0.0) for hd in heads]
        inv = [eye - a_ for a_ in a]
        pw = [_mm3(a_, a_) for a_ in a]
        inv = [inv[hd] + _mm3(inv[hd], pw[hd]) for hd in heads]
        n = 2
        while 2 * n < c:
            pw = [_mm3(p_, p_) for p_ in pw]
            inv = [inv[hd] + _mm3(inv[hd], pw[hd]) for hd in heads]
            n *= 2
        pre.append(dict(
            sol_v=[_mm3(inv[hd], v[hd] * bcol[hd]) for hd in heads],
            sol_k=[_mm3(inv[hd], k[hd] * (bcol[hd] * egc[hd])) for hd in heads],
            att=[qk[hd] * decay[hd] for hd in heads],
            qe=[q[hd] * egc[hd] for hd in heads],
            ke=[k[hd] * jnp.exp(gl[hd] - gc[hd]) for hd in heads],
            egl=[jnp.exp(gl[hd]) for hd in heads]))

    s = [s_scr[hd] for hd in heads]
    for ch, p_ in enumerate(pre):
        rows = slice(ch * c, (ch + 1) * c)
        u = [p_["sol_v"][hd] - _mm(p_["sol_k"][hd], s[hd]) for hd in heads]
        o = [_mm(p_["qe"][hd], s[hd]) + _mm(p_["att"][hd], u[hd]) for hd in heads]
        s = [s[hd] * p_["egl"][hd] + _mm(p_["ke"][hd], u[hd], _TN) for hd in heads]
        for hd in heads:
            o_ref[rows, lo[hd]:lo[hd] + HEAD128] = _rms_rows(o[hd], nw) * _silu(z_ref[rows, lo[hd]:lo[hd] + HEAD128])
    for hd in heads:
        s_scr[hd] = s[hd]

    @pl.when(j == pl.num_programs(1) - 1)
    def _():
        sout_ref[...] = s_scr[...]


def _gdn(pp, conv0, s0, cw, hp, nw, *, groups, t, r, c):
    nt = t // r
    cc = 3 * N_HEADS128 * HEAD128
    vw = N_HEADS128 * HEAD128
    return pl.pallas_call(
        functools.partial(_gdn_body, r=r, c=c),
        out_shape=(jax.ShapeDtypeStruct((groups * t, vw), F32),
                   jax.ShapeDtypeStruct((groups, N_HEADS128, HEAD128, HEAD128), F32)),
        grid=(groups, nt),
        in_specs=[pl.BlockSpec((r, cc), lambda g, i: (g * nt + i, 0)),
                  pl.BlockSpec((r, vw), lambda g, i: (g * nt + i, cc // vw)),
                  pl.BlockSpec((r, LANES), lambda g, i: (g * nt + i, (cc + vw + XA_WIDTH) // LANES)),
                  pl.BlockSpec((None, 8, cc), lambda g, i: (g, 0, 0)),
                  pl.BlockSpec((None, N_HEADS128, HEAD128, HEAD128), lambda g, i: (g, 0, 0, 0)),
                  pl.BlockSpec((4, cc), lambda g, i: (0, 0)),
                  pl.BlockSpec((2, LANES), lambda g, i: (0, 0)),
                  pl.BlockSpec((1, HEAD128), lambda g, i: (0, 0))],
        out_specs=(pl.BlockSpec((r, vw), lambda g, i: (g * nt + i, 0)),
                   pl.BlockSpec((None, N_HEADS128, HEAD128, HEAD128), lambda g, i: (g, 0, 0, 0))),
        scratch_shapes=[pltpu.VMEM((r + 8, cc), F32), pltpu.VMEM((r, cc), F32),
                        pltpu.VMEM((N_HEADS128, HEAD128, HEAD128), F32)],
        compiler_params=_cparams(("parallel", "arbitrary")),
        name="gdn",
    )(pp, pp, pp, conv0, s0, cw, hp, nw)


def _hgrn_body(q_ref, f_ref, i_ref, g_ref, s0_ref, lb_ref, nw_ref, o_ref, sout_ref, st_scr, *, r, c):
    j = pl.program_id(1)

    @pl.when(j == 0)
    def _():
        for hd in range(N_HEADS128):
            st_scr[hd] = s0_ref[hd].T

    nw = nw_ref[...]
    cum_m = _b01(_tri(c, "incl"))
    nsub = c // HG_SUB
    tri8 = _iota((HG_SUB, 1), 0)
    heads = range(N_HEADS128)
    lo = [hd * HEAD128 for hd in heads]
    lb = [lb_ref[:, l:l + HEAD128] for l in lo]
    log_lb = [jnp.log(x_) for x_ in lb]
    log_1m = [jnp.log1p(-x_) for x_ in lb]
    qe_all, upd_all, ebl_all, intra_all = [], [], [], []
    for ch in range(r // c):
        r0 = ch * c
        ff = [f_ref[r0:r0 + c, l:l + HEAD128] for l in lo]
        logf = [_logaddexp(log_lb[hd], log_1m[hd] + _log_sigmoid(ff[hd])) for hd in heads]
        k = [(1.0 - lb[hd]) * _sigmoid(-ff[hd]) for hd in heads]
        q = [_silu(q_ref[r0:r0 + c, l:l + HEAD128]) * (HEAD128 ** -0.5) for l in lo]
        v = [i_ref[r0:r0 + c, l:l + HEAD128] for l in lo]
        bc = [_sel_l(cum_m, x_) for x_ in logf]
        bl = [bc[hd][c - 1:c, :] for hd in heads]
        qe_all.append([q[hd] * jnp.exp(bc[hd]) for hd in heads])
        ebl_all.append([jnp.exp(bl[hd]) for hd in heads])
        upd_all.append([_mm(v[hd], k[hd] * jnp.exp(bl[hd] - bc[hd]), _TN) for hd in heads])
        blocks = [[None] for hd in heads]
        for sb in range(1, nsub):
            b0 = sb * HG_SUB
            for hd in heads:
                bref = bc[hd][b0 - 1:b0, :]
                qs = q[hd][b0:b0 + HG_SUB, :] * jnp.exp(bc[hd][b0:b0 + HG_SUB, :] - bref)
                ks = k[hd][0:b0, :] * jnp.exp(bref - bc[hd][0:b0, :])
                att = _mm(qs, ks, _NT)
                blocks[hd].append(_mm(att, v[hd][0:b0, :]))
        bc2 = [bc[hd] * LOG2E for hd in heads]
        for sb in range(nsub):
            b0 = sb * HG_SUB
            for hd in heads:
                qb, bcb = q[hd][b0:b0 + HG_SUB, :], bc2[hd][b0:b0 + HG_SUB, :]
                ob = blocks[hd][sb]
                for s_ in range(HG_SUB):
                    w = jnp.exp2(bcb - bc2[hd][b0 + s_:b0 + s_ + 1, :])
                    col = jnp.sum(qb * k[hd][b0 + s_:b0 + s_ + 1, :] * w, axis=-1, keepdims=True)
                    t_ = jnp.where(tri8 >= s_, col, 0.0) * v[hd][b0 + s_:b0 + s_ + 1, :]
                    ob = t_ if ob is None else ob + t_
                blocks[hd][sb] = ob
        intra_all.append([blocks[hd][0] if nsub == 1 else jnp.concatenate(blocks[hd], axis=0) for hd in heads])
    st = [st_scr[hd] for hd in heads]
    for ch in range(r // c):
        r0 = ch * c
        o = [intra_all[ch][hd] + _mm(qe_all[ch][hd], st[hd], _NT) for hd in heads]
        st = [st[hd] * ebl_all[ch][hd] + upd_all[ch][hd] for hd in heads]
        for hd in heads:
            o_ref[r0:r0 + c, lo[hd]:lo[hd] + HEAD128] = _rms_rows(o[hd], nw) * _silu(g_ref[r0:r0 + c, lo[hd]:lo[hd] + HEAD128])
    for hd in heads:
        st_scr[hd] = st[hd]

    @pl.when(j == pl.num_programs(1) - 1)
    def _():
        for hd in range(N_HEADS128):
            sout_ref[hd] = st_scr[hd].T


def _hgrn(pp, s0, lb, nw, *, groups, t, r, c, row_off):
    nt = t // r
    off = row_off // r
    w = N_HEADS128 * HEAD128
    rows = lambda col: pl.BlockSpec((r, w), lambda g, i: (off + g * nt + i, col))
    return pl.pallas_call(
        functools.partial(_hgrn_body, r=r, c=c),
        out_shape=(jax.ShapeDtypeStruct((groups * t, w), F32),
                   jax.ShapeDtypeStruct((groups, N_HEADS128, HEAD128, HEAD128), F32)),
        grid=(groups, nt),
        in_specs=[rows(0), rows(1), rows(2), rows(3),
                  pl.BlockSpec((None, N_HEADS128, HEAD128, HEAD128), lambda g, i: (g, 0, 0, 0)),
                  pl.BlockSpec((1, w), lambda g, i: (0, 0)),
                  pl.BlockSpec((1, HEAD128), lambda g, i: (0, 0))],
        out_specs=(pl.BlockSpec((r, w), lambda g, i: (g * nt + i, 0)),
                   pl.BlockSpec((None, N_HEADS128, HEAD128, HEAD128), lambda g, i: (g, 0, 0, 0))),
        scratch_shapes=[pltpu.VMEM((N_HEADS128, HEAD128, HEAD128), F32)],
        compiler_params=_cparams(("parallel", "arbitrary")),
        name="hgrn2",
    )(pp, pp, pp, pp, s0, lb, nw)


def _fox_prep_body(q_ref, k_ref, fl_ref, qn_ref, kn_ref, fb_ref, bd_ref, qo_ref, ko_ref, lf_ref, fc_ref, carry):
    j = pl.program_id(1)

    @pl.when(j == 0)
    def _():
        carry[...] = jnp.zeros_like(carry)

    bd = bd_ref[...]
    qo_ref[...] = _group_rms(q_ref[...], qn_ref[...], bd, FOX_DIM)
    ko_ref[...] = _group_rms(k_ref[...], kn_ref[...], bd, FOX_DIM)
    lf = jax.nn.log_sigmoid(fl_ref[...] + fb_ref[...])
    lf_ref[...] = lf
    tr = lf.shape[0]
    fc = _sel_l(_b01(_tri(tr, "incl")), lf) + carry[...]
    fc_ref[...] = fc
    carry[...] = fc[tr - 1:tr, :]


def _fox_prep(pp, qn_t, kn_t, fb, bd64, *, groups, t, tr, row_off):
    nt = t // tr
    off = row_off // tr
    w = FOX_HEADS * FOX_DIM
    n = groups * t
    big = jax.ShapeDtypeStruct((n, w), F32)
    small = jax.ShapeDtypeStruct((n, LANES), F32)
    return pl.pallas_call(
        _fox_prep_body,
        out_shape=(big, big, small, small),
        grid=(groups, nt),
        in_specs=[pl.BlockSpec((tr, w), lambda g, i: (off + g * nt + i, 0)),
                  pl.BlockSpec((tr, w), lambda g, i: (off + g * nt + i, 1)),
                  pl.BlockSpec((tr, LANES), lambda g, i: (off + g * nt + i, (4 * w + XA_WIDTH) // LANES)),
                  pl.BlockSpec((1, w), lambda g, i: (0, 0)),
                  pl.BlockSpec((1, w), lambda g, i: (0, 0)),
                  pl.BlockSpec((1, LANES), lambda g, i: (0, 0)),
                  pl.BlockSpec((256, 256), lambda g, i: (0, 0))],
        out_specs=(pl.BlockSpec((tr, w), lambda g, i: (g * nt + i, 0)),
                   pl.BlockSpec((tr, w), lambda g, i: (g * nt + i, 0)),
                   pl.BlockSpec((tr, LANES), lambda g, i: (g * nt + i, 0)),
                   pl.BlockSpec((tr, LANES), lambda g, i: (g * nt + i, 0))),
        scratch_shapes=[pltpu.VMEM((1, LANES), F32)],
        compiler_params=_cparams(("parallel", "arbitrary")),
        name="fox_prep",
    )(pp, pp, pp, qn_t, kn_t, fb, bd64)


def _fox_prep_t_body(q_ref, k_ref, v_ref, fl_ref, qn_ref, kn_ref, fb_ref, bd_ref,
                     qo_ref, kt_ref, vt_ref, lft_ref, fc_ref, kb_ref, carry):
    j = pl.program_id(1)

    @pl.when(j == 0)
    def _():
        carry[...] = jnp.zeros_like(carry)

    bd = bd_ref[...]
    qo_ref[...] = _group_rms(q_ref[...], qn_ref[...], bd, FOX_DIM)
    kt_ref[...] = _group_rms(k_ref[...], kn_ref[...], bd, FOX_DIM).T
    vt_ref[...] = v_ref[...].T
    lf = jax.nn.log_sigmoid(fl_ref[...] + fb_ref[...])
    tr = lf.shape[0]
    fc = _sel_l(_b01(_tri(tr, "incl")), lf) + carry[...]
    fc_ref[...] = fc
    carry[...] = fc[tr - 1:tr, :]
    lft_ref[...] = lf.T[0:FOX_HEADS, :]
    rr = _iota((LANES, LANES), 0) % FOX_BIAS_ROWS
    first = 2 * (_iota((LANES, LANES), 0) // FOX_BIAS_ROWS)
    lanei = _iota((LANES, LANES), 1)
    kbias = jnp.where((_iota((LANES, 1), 0) % FOX_BIAS_ROWS) // 3 == 1, 1.0, 0.0)
    for pi, part in enumerate(_split3(fc * LOG2E)):
        pick = ((rr == pi) & (lanei == first)) | ((rr == 6 + pi) & (lanei == first + 1))
        kbias = kbias + lax.dot_general(jnp.where(pick, -1.0, 0.0).astype(BF16), part, _NT, preferred_element_type=F32)
    kb_ref[...] = kbias.astype(BF16)


def _fox_prep_t(pp, qn_t, kn_t, fb, bd64, *, groups, t, tr):
    nt = t // tr
    w = FOX_HEADS * FOX_DIM
    n = groups * t
    rows = lambda col: pl.BlockSpec((tr, w), lambda g, i: (g * nt + i, col))
    const = lambda g, i: (0, 0)
    tmaj = jax.ShapeDtypeStruct((groups, w, t), F32)
    hmaj = jax.ShapeDtypeStruct((groups, FOX_HEADS, t), F32)
    return pl.pallas_call(
        _fox_prep_t_body,
        out_shape=(jax.ShapeDtypeStruct((n, w), F32), tmaj, tmaj, hmaj, jax.ShapeDtypeStruct((n, LANES), F32),
                   jax.ShapeDtypeStruct((groups, LANES, t), BF16)),
        grid=(groups, nt),
        in_specs=[rows(0), rows(1), rows(2),
                  pl.BlockSpec((tr, LANES), lambda g, i: (g * nt + i, (4 * w + XA_WIDTH) // LANES)),
                  pl.BlockSpec((1, w), const), pl.BlockSpec((1, w), const), pl.BlockSpec((1, LANES), const),
                  pl.BlockSpec((256, 256), const)],
        out_specs=(pl.BlockSpec((tr, w), lambda g, i: (g * nt + i, 0)),
                   pl.BlockSpec((None, w, tr), lambda g, i: (g, 0, i)),
                   pl.BlockSpec((None, w, tr), lambda g, i: (g, 0, i)),
                   pl.BlockSpec((None, FOX_HEADS, tr), lambda g, i: (g, 0, i)),
                   pl.BlockSpec((tr, LANES), lambda g, i: (g * nt + i, 0)),
                   pl.BlockSpec((None, LANES, tr), lambda g, i: (g, 0, i))),
        scratch_shapes=[pltpu.VMEM((1, LANES), F32)],
        compiler_params=_cparams(("parallel", "arbitrary")),
        name="fox_prep_t",
    )(pp, pp, pp, pp, qn_t, kn_t, fb, bd64)


def _fox_flash_body(q_ref, k_ref, kb_ref, v_ref, og_ref, fq_ref, o_ref, m_scr, l_scr, acc_scr, *, tq):
    hp = pl.program_id(1)
    qi = pl.program_id(2)
    lane = _iota((1, LANES), 1)
    q = q_ref[...] * (FOX_DIM ** -0.5 * LOG2E)
    fq_parts = _split3(fq_ref[...] * LOG2E)
    brow = _iota((LANES, FOX_BIAS_ROWS), 0)
    bcol = _iota((LANES, FOX_BIAS_ROWS), 1)
    ones_at = _iota((1, FOX_BIAS_ROWS), 1) // 3
    qs = []
    for j in range(2):
        qb = jnp.where(ones_at == 2 * j, 1.0, 0.0)
        for pi, part in enumerate(fq_parts):
            pick = _b01((brow == 2 * hp + j) & (bcol == 3 + pi))
            qb = qb + jnp.dot(part, pick, preferred_element_type=F32)
        qm = jnp.where((lane // FOX_DIM) == j, q, 0.0).astype(BF16)
        qs.append(jnp.concatenate([qm, qb.astype(BF16)], axis=1))
        m_scr[j] = jnp.full((tq, LANES), -jnp.inf, F32)
        l_scr[j] = jnp.zeros((tq, LANES), F32)
        acc_scr[j] = jnp.zeros((tq, LANES), F32)

    def rep(x, n):
        return x if n == LANES else jnp.concatenate([x] * (n // LANES), axis=1)

    def block(ki, masked):
        k0 = pl.multiple_of(ki * tq, tq)
        kb = jnp.concatenate([k_ref[:, pl.ds(k0, tq)].astype(BF16),
                              kb_ref[:, pl.ds(k0, tq)]], axis=0)
        vb = v_ref[pl.ds(k0, tq), :].astype(BF16)
        chains = [(j, r0) for r0 in range(0, tq, FOX_RG) for j in range(2)]
        nk = [r0 + FOX_RG if masked else tq for (_, r0) in chains]
        s = [jnp.dot(qs[j][r0:r0 + FOX_RG, :], kb[:, :n_], preferred_element_type=F32)
             for (j, r0), n_ in zip(chains, nk)]
        if masked:
            s = [jnp.where(_iota((FOX_RG, n_), 1) <= _iota((FOX_RG, n_), 0) + r0, s_, -jnp.inf)
                 for (_, r0), n_, s_ in zip(chains, nk, s)]
        m_old = [m_scr[j, r0:r0 + FOX_RG, :] for (j, r0) in chains]
        m_new = [jnp.maximum(mo_, jnp.max(s_, axis=-1, keepdims=True)) for mo_, s_ in zip(m_old, s)]
        alpha = [jnp.exp2(mo_ - mn_) for mo_, mn_ in zip(m_old, m_new)]
        p = [jnp.exp2(s_ - rep(mn_, n_)) for s_, mn_, n_ in zip(s, m_new, nk)]
        pv = [jnp.dot(p_.astype(BF16), vb[:n_, :], preferred_element_type=F32) for p_, n_ in zip(p, nk)]
        for c_, (j, r0) in enumerate(chains):
            rows = slice(r0, r0 + FOX_RG)
            l_scr[j, rows, :] = alpha[c_] * l_scr[j, rows, :] + jnp.sum(p[c_], axis=-1, keepdims=True)
            acc_scr[j, rows, :] = alpha[c_] * acc_scr[j, rows, :] + pv[c_]
            m_scr[j, rows, :] = m_new[c_]

    def body(ki, carry):
        block(ki, False)
        return carry

    lax.fori_loop(0, qi, body, 0)
    block(qi, True)
    o0 = acc_scr[0] / l_scr[0]
    o1 = acc_scr[1] / l_scr[1]
    o_ref[...] = jnp.where(lane < FOX_DIM, o0, o1) * _sigmoid(og_ref[...])


def _fox_flash(qn, kt, kbias, pp, fc, *, groups, t, tq):
    nq = t // tq
    w = FOX_HEADS * FOX_DIM
    hp_n = w // LANES
    vblk = 2 * w // LANES
    gblk = 3 * w // LANES
    return pl.pallas_call(
        functools.partial(_fox_flash_body, tq=tq),
        out_shape=jax.ShapeDtypeStruct((groups * t, w), F32),
        grid=(groups, hp_n, nq),
        in_specs=[pl.BlockSpec((tq, LANES), lambda g, h, i: (g * nq + i, h)),
                  pl.BlockSpec((None, LANES, t), lambda g, h, i: (g, h, 0)),
                  pl.BlockSpec((None, FOX_BIAS_ROWS, t), lambda g, h, i: (g, h, 0)),
                  pl.BlockSpec((t, LANES), lambda g, h, i: (g, vblk + h)),
                  pl.BlockSpec((tq, LANES), lambda g, h, i: (g * nq + i, gblk + h)),
                  pl.BlockSpec((tq, LANES), lambda g, h, i: (g * nq + i, 0))],
        out_specs=pl.BlockSpec((tq, LANES), lambda g, h, i: (g * nq + i, h)),
        scratch_shapes=[pltpu.VMEM((2, tq, LANES), F32)] * 3,
        compiler_params=_cparams(("parallel", "parallel", "arbitrary")),
        name="fox_flash",
    )(qn, kt, kbias, pp, pp, fc)


def _fox_sample_body(pt_ref, *refs, gp, tnew):
    del pt_ref
    k_refs = refs[0:gp]
    v_refs = refs[gp:2 * gp]
    lf_refs = refs[2 * gp:3 * gp]
    q_ref, kn_ref, vn_ref, og_ref, lfn_ref, o_ref, q_scr, m_scr, l_scr, acc, carry = refs[3 * gp:]
    pg = pl.program_id(1)

    @pl.when(pg == 0)
    def _():
        qs = q_ref[...] * (FOX_DIM ** -0.5)
        for hd in range(FOX_HEADS):
            q_scr[hd] = qs[:, hd * FOX_DIM:(hd + 1) * FOX_DIM]
        m_scr[...] = jnp.full(m_scr.shape, -jnp.inf, F32)
        l_scr[...] = jnp.zeros_like(l_scr)
        acc[...] = jnp.zeros_like(acc)
        carry[...] = jnp.zeros_like(carry)

    fnew = _sel_l(_b01(_tri(tnew, "incl")), lfn_ref[...])

    heads = range(FOX_HEADS)

    def update(s, pv_fn):
        m_old = [m_scr[hd] for hd in heads]
        m_new = [jnp.maximum(m_old[hd], jnp.max(s[hd], axis=-1, keepdims=True)) for hd in heads]
        alpha = [jnp.exp(m_old[hd] - m_new[hd]) for hd in heads]
        p = [jnp.exp(s[hd] - m_new[hd]) for hd in heads]
        pv = [pv_fn(hd, p[hd].astype(BF16)) for hd in heads]
        for hd in heads:
            l_scr[hd] = alpha[hd] * l_scr[hd] + jnp.sum(p[hd], axis=-1, keepdims=True)
            acc[hd] = alpha[hd] * acc[hd] + pv[hd]
            m_scr[hd] = m_new[hd]

    later = _b01(_tri(PAGE, "strict"))
    rests = []
    run = carry[...]
    for j in range(gp):
        lft = lf_refs[j][...]
        rests.append(_sel_r(lft, later) + run)
        run = run + jnp.sum(lft, axis=-1, keepdims=True)
    carry[...] = run

    qh = [q_scr[hd].astype(BF16) for hd in heads]
    s = [jnp.concatenate(
        [jnp.dot(qh[hd], k_refs[j][hd].astype(BF16), preferred_element_type=F32) + rests[j][hd:hd + 1, :]
         for j in range(gp)], axis=1) + fnew[:, hd:hd + 1] for hd in heads]

    def pv_past(hd, p):
        parts = [lax.dot_general(p[:, j * PAGE:(j + 1) * PAGE], v_refs[j][hd].astype(BF16), _NT,
                                 preferred_element_type=F32) for j in range(gp)]
        while len(parts) > 1:
            parts = [parts[i] + parts[i + 1] for i in range(0, len(parts), 2)]
        return parts[0]

    update(s, pv_past)

    @pl.when(pg == pl.num_programs(1) - 1)
    def _():
        fnew_t = fnew.T
        causal = _tri(tnew, "incl")
        lo = [hd * FOX_DIM for hd in heads]
        qh = [q_scr[hd].astype(BF16) for hd in heads]
        s = [lax.dot_general(qh[hd], kn_ref[:, lo[hd]:lo[hd] + FOX_DIM].astype(BF16), _NT, preferred_element_type=F32)
             + fnew[:, hd:hd + 1] - fnew_t[hd:hd + 1, :] for hd in heads]
        s = [jnp.where(causal, s_, -jnp.inf) for s_ in s]
        update(s, lambda hd, p: jnp.dot(p, vn_ref[:, lo[hd]:lo[hd] + FOX_DIM].astype(BF16), preferred_element_type=F32))
        o_ref[...] = jnp.concatenate([acc[hd] / l_scr[hd] for hd in heads], axis=1) * _sigmoid(og_ref[...])


def _fox_sample(page_table, ckt, cvt, clft, qn, kn, pp, lfn, *, tnew, row_off, gp):
    ns, npages = page_table.shape
    w = FOX_HEADS * FOX_DIM
    ngrp = npages // gp
    off = row_off // tnew

    def page_map(j, nd):
        return lambda s, g, pt: (pt[s, npages - 1 - (g * gp + j)],) + (0,) * nd

    kv_specs = [pl.BlockSpec((None, FOX_HEADS, FOX_DIM, PAGE), page_map(j, 3)) for j in range(gp)]
    lf_specs = [pl.BlockSpec((None, FOX_HEADS, PAGE), page_map(j, 2)) for j in range(gp)]
    row = lambda s, g, pt: (s, 0)
    grid_spec = pltpu.PrefetchScalarGridSpec(
        num_scalar_prefetch=1,
        grid=(ns, ngrp),
        in_specs=kv_specs + kv_specs + lf_specs + [
            pl.BlockSpec((tnew, w), row),
            pl.BlockSpec((tnew, w), row),
            pl.BlockSpec((tnew, w), lambda s, g, pt: (off + s, 2)),
            pl.BlockSpec((tnew, w), lambda s, g, pt: (off + s, 3)),
            pl.BlockSpec((tnew, LANES), row)],
        out_specs=pl.BlockSpec((tnew, w), row),
        scratch_shapes=[pltpu.VMEM((FOX_HEADS, tnew, FOX_DIM), F32), pltpu.VMEM((FOX_HEADS, tnew, 1), F32),
                        pltpu.VMEM((FOX_HEADS, tnew, 1), F32), pltpu.VMEM((FOX_HEADS, tnew, FOX_DIM), F32),
                        pltpu.VMEM((FOX_HEADS, 1), F32)],
    )
    return pl.pallas_call(
        functools.partial(_fox_sample_body, gp=gp, tnew=tnew),
        out_shape=jax.ShapeDtypeStruct((ns * tnew, w), F32),
        grid_spec=grid_spec,
        compiler_params=_cparams(("parallel", "arbitrary")),
        name="fox_sample",
    )(page_table, *([ckt] * gp), *([cvt] * gp), *([clft] * gp), qn, kn, pp, pp, lfn)


def _cmlp_body(u_ref, v_ref, lg_ref, lbias_ref, ws_ref, bsb_ref, o_ref, vo_ref):
    u = jax.nn.gelu(u_ref[...], approximate=True)
    z = jax.nn.gelu(v_ref[...], approximate=True)
    mu = jnp.mean(z, axis=-1, keepdims=True)
    zc = z - mu
    v = zc * lax.rsqrt(jnp.mean(zc * zc, axis=-1, keepdims=True) + EPS) * lg_ref[...] + lbias_ref[...]
    vo_ref[...] = v
    tril = _tri(ws_ref.shape[1], "incl")
    gd = v.shape[1] // CM_GROUPS
    for g in range(CM_GROUPS):
        wm = jnp.where(tril, ws_ref[g], 0.0)
        mixed = _mm(wm, v[:, g * gd:(g + 1) * gd]) + bsb_ref[g]
        o_ref[:, g * gd:(g + 1) * gd] = u[:, g * gd:(g + 1) * gd] * mixed


def _chunk_mlp(pp, ln_g, ln_b, ws, bsb, *, nrows, row_off):
    w = ln_g.shape[1]
    cr = ws.shape[1]
    off = row_off // cr
    out = jax.ShapeDtypeStruct((nrows, w), F32)
    const2 = lambda i: (0, 0)
    const3 = lambda i: (0, 0, 0)
    return pl.pallas_call(
        _cmlp_body,
        out_shape=(out, out),
        grid=(nrows // cr,),
        in_specs=[pl.BlockSpec((cr, w), lambda i: (off + i, 0)),
                  pl.BlockSpec((cr, w), lambda i: (off + i, 1)),
                  pl.BlockSpec((1, w), const2), pl.BlockSpec((1, w), const2),
                  pl.BlockSpec((CM_GROUPS, cr, cr), const3),
                  pl.BlockSpec((CM_GROUPS, cr, w // CM_GROUPS), const3)],
        out_specs=(pl.BlockSpec((cr, w), lambda i: (i, 0)),
                   pl.BlockSpec((cr, w), lambda i: (i, 0))),
        compiler_params=_cparams(("parallel",)),
        name="chunk_mlp",
    )(pp, pp, ln_g, ln_b, ws, bsb)


def _pad_cols(w, mult):
    pad = (-w.shape[1]) % mult
    return jnp.pad(w, ((0, 0), (0, pad))) if pad else w


def _lane_row(v, start=0):
    return jnp.zeros((1, LANES), F32).at[0, start:start + v.shape[0]].set(v.astype(F32))


def kernel(x_prompt, x_sample, mem_prompt, state_a_conv, state_a_ssm, state_b_ssm, cache_c_k, cache_c_v, cache_c_logf, cache_mem_k, cache_mem_v, page_table, norm_mix, w_out, norm_mlp, w_up, w_down, mem_norm, w_mem_kv, xa_qnorm, xa_knorm, w_in_a, a_conv_w, a_log, a_dt_bias, a_norm_w, w_in_b, hg_lb, b_norm_w, w_in_c, c_fbias, c_qnorm, c_knorm, w_in_d, d_ln_g, d_ln_b, d_ws, d_bs):
    bp, seq, d = x_prompt.shape
    ds, dseq, _ = x_sample.shape
    depth = norm_mix.shape[0]
    np_rows, ns_rows = bp * seq, ds * dseq
    tm_p = 512 if np_rows % 512 == 0 else 128
    tm_s = 256 if ns_rows % 256 == 0 else 8
    w1024 = N_HEADS128 * HEAD128
    cc = 3 * w1024

    h_p, h_s = x_prompt.reshape(np_rows, d), x_sample.reshape(ns_rows, d)
    bd64 = (jnp.arange(256)[:, None] // 64 == jnp.arange(256)[None, :] // 64).astype(BF16)

    mk_all, mv_all = _memory_kv(mem_prompt, mem_norm[:, None, :], jnp.swapaxes(w_mem_kv, 1, 2).astype(BF16),
                                xa_knorm[:, :, None])

    lb_w = jax.nn.softmax(hg_lb.astype(F32), axis=0)
    lower_bounds = jnp.cumsum(lb_w, axis=0) - lb_w[0]

    outs = {}
    for l in range(depth):
        kind, j = l % 4, l // 4
        if kind == 0:
            wi = w_in_a[j]
            w_packed = jnp.concatenate([wi[:, :cc + w1024], wi[:, cc + w1024 + 16:], _pad_cols(wi[:, cc + w1024:cc + w1024 + 16], LANES)], axis=1)
            xq_blk = (cc + w1024) // XA_WIDTH
        elif kind == 1:
            w_packed = w_in_b[j]
            xq_blk = 4 * w1024 // XA_WIDTH
        elif kind == 2:
            wi = w_in_c[j]
            w_packed = jnp.concatenate([wi[:, :4 * w1024], wi[:, 4 * w1024 + 16:], _pad_cols(wi[:, 4 * w1024:4 * w1024 + 16], LANES)], axis=1)
            xq_blk = 4 * w1024 // XA_WIDTH
        else:
            w_packed = w_in_d[j]
            xq_blk = 2 * w1024 // XA_WIDTH
        w_bf = w_packed.astype(BF16)
        g_mix = norm_mix[l][None, :]
        pp_p = _norm_proj(h_p, g_mix, w_bf, tm=tm_p)
        pp_s = _norm_proj(h_s, g_mix, w_bf, tm=tm_s)

        qn_t = jnp.tile(xa_qnorm[l], XA_HEADS)[None, :]
        xo_p = _mem_attend(pp_p, mk_all[l], mv_all[l], qn_t, bd64, groups=bp, t=seq, tt=512, gs=1, rg=128, col_blk=xq_blk)
        to_hdn = lambda a: jnp.transpose(a, (0, 2, 3, 1)).reshape(ds, XA_WIDTH, N_MEM)
        xo_s = _mem_attend(pp_s, to_hdn(cache_mem_k[l]), to_hdn(cache_mem_v[l]),
                           qn_t, bd64, groups=ds, t=dseq, tt=dseq, gs=4, rg=dseq, col_blk=xq_blk)

        if kind == 0:
            hp = jnp.concatenate([_lane_row(a_log[j], 8), _lane_row(a_dt_bias[j], 8)], axis=0)
            nw = a_norm_w[j][None, :]
            conv0_p = jnp.zeros((bp, 8, cc), F32)
            conv0_s = jnp.pad(state_a_conv[j], ((0, 0), (5, 0), (0, 0)))
            mo_p, st_p = _gdn(pp_p, conv0_p, jnp.zeros((bp, N_HEADS128, HEAD128, HEAD128), F32), a_conv_w[j], hp, nw,
                              groups=bp, t=seq, r=GDN_ROWS, c=GDN_CHUNK)
            mo_s, st_s = _gdn(pp_s, conv0_s, state_a_ssm[j], a_conv_w[j], hp, nw,
                              groups=ds, t=dseq, r=dseq, c=dseq)
            outs.setdefault("a_conv_p", []).append(pp_p.reshape(bp, seq, -1)[:, seq - 3:, :cc])
            outs.setdefault("a_conv_s", []).append(pp_s.reshape(ds, dseq, -1)[:, dseq - 3:, :cc])
            outs.setdefault("a_ssm_p", []).append(st_p)
            outs.setdefault("a_ssm_s", []).append(st_s)
        elif kind == 1:
            lb = lower_bounds[l][None, :]
            nw = b_norm_w[j][None, :]
            mo_p, st_p = _hgrn(pp_p, jnp.zeros((bp, N_HEADS128, HEAD128, HEAD128), F32), lb, nw,
                               groups=bp, t=seq, r=128, c=HG_CHUNK, row_off=0)
            mo_s, st_s = _hgrn(pp_s, state_b_ssm[j], lb, nw, groups=ds, t=dseq, r=dseq, c=dseq, row_off=0)
            outs.setdefault("b_ssm_p", []).append(st_p)
            outs.setdefault("b_ssm_s", []).append(st_s)
        elif kind == 2:
            qn_f = jnp.tile(c_qnorm[j], FOX_HEADS)[None, :]
            kn_f = jnp.tile(c_knorm[j], FOX_HEADS)[None, :]
            fb = _lane_row(c_fbias[j])
            q_p, kt_p, vt_p, lft_p, fc_p, kb_p = _fox_prep_t(pp_p, qn_f, kn_f, fb, bd64, groups=bp, t=seq, tr=256)
            q_s, k_s, lf_s, _ = _fox_prep(pp_s, qn_f, kn_f, fb, bd64, groups=ds, t=dseq, tr=dseq, row_off=0)
            mo_p = _fox_flash(q_p, kt_p, kb_p, pp_p, fc_p, groups=bp, t=seq, tq=FOX_TQ)
            pos_minor = lambda a: jnp.transpose(a, (0, 2, 3, 1))
            mo_s = _fox_sample(page_table, pos_minor(cache_c_k[j]), pos_minor(cache_c_v[j]),
                               jnp.swapaxes(cache_c_logf[j], 1, 2), q_s, k_s, pp_s, lf_s,
                               tnew=dseq, row_off=0, gp=PAGES_PER_STEP)
            time_major = lambda a: jnp.transpose(a.reshape(bp, FOX_HEADS, FOX_DIM, seq), (0, 3, 1, 2))
            outs.setdefault("c_k_p", []).append(time_major(kt_p))
            outs.setdefault("c_v_p", []).append(time_major(vt_p))
            outs.setdefault("c_lf_p", []).append(jnp.swapaxes(lft_p, 1, 2))
            outs.setdefault("c_k_s", []).append(k_s.reshape(ds, dseq, FOX_HEADS, FOX_DIM))
            outs.setdefault("c_v_s", []).append(pp_s[:, 2 * w1024:3 * w1024].reshape(ds, dseq, FOX_HEADS, FOX_DIM))
            outs.setdefault("c_lf_s", []).append(lf_s[:, :FOX_HEADS].reshape(ds, dseq, FOX_HEADS))
        else:
            gd = w1024 // CM_GROUPS
            lg, lbias = d_ln_g[j][None, :], d_ln_b[j][None, :]
            mo_p, _ = _chunk_mlp(pp_p, lg, lbias, d_ws[j], jnp.broadcast_to(d_bs[j][:, :, None], (CM_GROUPS, CM_CHUNK, gd)),
                                 nrows=np_rows, row_off=0)
            mo_s, v_s = _chunk_mlp(pp_s, lg, lbias, d_ws[j][:, :dseq, :dseq],
                                   jnp.broadcast_to(d_bs[j][:, :dseq, None], (CM_GROUPS, dseq, gd)), nrows=ns_rows, row_off=0)
            outs.setdefault("d_v_s", []).append(v_s.reshape(ds, dseq, w1024))

        wo = w_out[l].astype(BF16)
        mlp_w = (wo[:w1024], wo[w1024:], norm_mlp[l][None, :], w_up[l].astype(BF16), w_down[l].astype(BF16))
        h_p = _out_mlp(h_p, mo_p, xo_p, *mlp_w, tm=tm_p)
        h_s = _out_mlp(h_s, mo_s, xo_s, *mlp_w, tm=tm_s)

    st = lambda name: jnp.stack(outs[name])
    mem_out = lambda a: jnp.transpose(a.reshape(depth, bp, XA_HEADS, XA_DIM, N_MEM), (0, 1, 4, 2, 3))
    return (h_p.reshape(bp, seq, d), h_s.reshape(ds, dseq, d),
            st("a_conv_p"), st("a_conv_s"), st("a_ssm_p"), st("a_ssm_s"),
            st("b_ssm_p"), st("b_ssm_s"),
            st("c_k_p"), st("c_v_p"), st("c_lf_p"), st("c_k_s"), st("c_v_s"), st("c_lf_s"),
            st("d_v_s"), mem_out(mk_all), mem_out(mv_all))
```

```python
import functools
import math

import jax
import jax.numpy as jnp
from jax import lax
from jax.experimental import pallas as pl
from jax.experimental.pallas import tpu as pltpu

F32 = jnp.float32
BF16 = jnp.bfloat16
EPS = 1e-6
LOG2E = 1.4426950408889634

LANES = 128
SUBLANES = 8
VMEM_LIMIT = 56 * 1024 * 1024

XA_HEADS, XA_DIM, N_MEM = 4, 64, 256
XA_WIDTH = XA_HEADS * XA_DIM
HEAD128 = 128
N_HEADS128 = 8
FOX_DIM, FOX_HEADS = 64, 16
PAGE = 128
CM_CHUNK, CM_GROUPS = 128, 8
GDN_CHUNK, HG_CHUNK, HG_SUB = 64, 32, 8
GDN_ROWS = 256
HG_ROWS = 256
FOX_TQ = 512
FOX_RG = 128
FOX_BIAS_ROWS = 16
PAGES_PER_STEP = 16


def _cparams(sem):
    return pltpu.CompilerParams(dimension_semantics=sem, vmem_limit_bytes=VMEM_LIMIT)


def _iota(shape, dim):
    return lax.broadcasted_iota(jnp.int32, shape, dim)


def _split3(x):
    hi = x.astype(BF16)
    r = x - hi.astype(F32)
    mid = r.astype(BF16)
    lo = (r - mid.astype(F32)).astype(BF16)
    return hi, mid, lo


def _split2(x):
    hi = x.astype(BF16)
    lo = (x - hi.astype(F32)).astype(BF16)
    return hi, lo


_NN = (((1,), (0,)), ((), ()))
_NT = (((1,), (1,)), ((), ()))
_TN = (((0,), (0,)), ((), ()))


def _mm(a, b, dims=_NN):
    return lax.dot_general(a.astype(BF16), b.astype(BF16), dims, preferred_element_type=F32)


def _sel_l(m, x, dims=_NN):
    out = None
    for p in _split3(x):
        t = lax.dot_general(m, p, dims, preferred_element_type=F32)
        out = t if out is None else out + t
    return out


def _sel_r(x, m, dims=_NN):
    out = None
    for p in _split3(x):
        t = lax.dot_general(p, m, dims, preferred_element_type=F32)
        out = t if out is None else out + t
    return out


def _mm3(a, b):
    ah, al = _split2(a)
    bh, bl = _split2(b)
    d = functools.partial(jnp.dot, preferred_element_type=F32)
    return d(ah, bh) + d(ah, bl) + d(al, bh)


def _sigmoid(x):
    return jax.nn.sigmoid(x)


def _silu(x):
    return x * jax.nn.sigmoid(x)


def _log_sigmoid(x):
    return jnp.minimum(x, 0.0) - jnp.log1p(jnp.exp(-jnp.abs(x)))


def _logaddexp(a, b):
    return jnp.maximum(a, b) + jnp.log1p(jnp.exp(-jnp.abs(a - b)))


def _rms_rows(x, g):
    return x * lax.rsqrt(jnp.mean(x * x, axis=-1, keepdims=True) + EPS) * g


def _group_rms(x, gain, bd, gsz):
    w = x.shape[1]
    x2 = x * x
    parts = []
    for c in range(0, w, 256):
        parts.append(_sel_r(x2[:, c:c + 256], bd))
    ms = (parts[0] if len(parts) == 1 else jnp.concatenate(parts, axis=1)) * (1.0 / gsz)
    return x * lax.rsqrt(ms + EPS) * gain


def _tri(n, kind):
    r, c = _iota((n, n), 0), _iota((n, n), 1)
    if kind == "incl":
        return r >= c
    if kind == "strict":
        return r > c
    if kind == "upper_incl":
        return r <= c
    if kind == "upper_strict":
        return r < c
    raise ValueError(kind)


def _b01(mask):
    return jnp.where(mask, 1.0, 0.0).astype(BF16)


def _proj_body(x_ref, g_ref, w_ref, o_ref, *, tn):
    xn = _rms_rows(x_ref[...], g_ref[...]).astype(BF16)
    wp = w_ref.shape[1]
    for c in range(0, wp, tn):
        e = min(c + tn, wp)
        o_ref[:, c:e] = jnp.dot(xn, w_ref[:, c:e], preferred_element_type=F32)


def _norm_proj(h, g, w, *, tm, tn=512):
    n, d = h.shape
    wp = w.shape[1]
    return pl.pallas_call(
        functools.partial(_proj_body, tn=tn),
        out_shape=jax.ShapeDtypeStruct((n, wp), F32),
        grid=(n // tm,),
        in_specs=[pl.BlockSpec((tm, d), lambda i: (i, 0)),
                  pl.BlockSpec((1, d), lambda i: (0, 0)),
                  pl.BlockSpec((d, wp), lambda i: (0, 0), pipeline_mode=pl.Buffered(1))],
        out_specs=pl.BlockSpec((tm, wp), lambda i: (i, 0)),
        compiler_params=_cparams(("parallel",)),
        name="norm_proj",
    )(h, g, w)


def _out_mlp_body(h_ref, mo_ref, xo_ref, wo1_ref, wo2_ref, g_ref, wu_ref, wd_ref, o_ref, *, tf):
    h2 = (h_ref[...]
          + jnp.dot(mo_ref[...].astype(BF16), wo1_ref[...], preferred_element_type=F32)
          + jnp.dot(xo_ref[...].astype(BF16), wo2_ref[...], preferred_element_type=F32))
    xn = _rms_rows(h2, g_ref[...]).astype(BF16)
    acc = None
    dff = wu_ref.shape[1]
    for c in range(0, dff, tf):
        up = jnp.dot(xn, wu_ref[:, c:c + tf], preferred_element_type=F32)
        act = jnp.square(jnp.maximum(up, 0.0)).astype(BF16)
        t = jnp.dot(act, wd_ref[c:c + tf, :], preferred_element_type=F32)
        acc = t if acc is None else acc + t
    o_ref[...] = h2 + acc


def _out_mlp(h, mo, xo, wo1, wo2, g, wu, wd, *, tm, tf=512):
    n, d = h.shape
    dff = wu.shape[1]
    xw = xo.shape[1]
    const = lambda i: (0, 0)
    one = pl.Buffered(1)
    return pl.pallas_call(
        functools.partial(_out_mlp_body, tf=tf),
        out_shape=jax.ShapeDtypeStruct((n, d), F32),
        grid=(n // tm,),
        in_specs=[pl.BlockSpec((tm, d), lambda i: (i, 0)),
                  pl.BlockSpec((tm, d), lambda i: (i, 0)),
                  pl.BlockSpec((tm, xw), lambda i: (i, 0)),
                  pl.BlockSpec((d, d), const, pipeline_mode=one),
                  pl.BlockSpec((xw, d), const, pipeline_mode=one),
                  pl.BlockSpec((1, d), const),
                  pl.BlockSpec((d, dff), const, pipeline_mode=one),
                  pl.BlockSpec((dff, d), const, pipeline_mode=one)],
        out_specs=pl.BlockSpec((tm, d), lambda i: (i, 0)),
        compiler_params=_cparams(("parallel",)),
        name="out_mlp",
    )(h, mo, xo, wo1, wo2, g, wu, wd)


def _memkv_body(x_ref, g_ref, wt_ref, kn_ref, k_ref, v_ref):
    xn = _rms_rows(x_ref[...], g_ref[...]).astype(BF16)
    kvt = lax.dot_general(wt_ref[...], xn, _NT, preferred_element_type=F32)
    kn = kn_ref[...]
    for hd in range(XA_HEADS):
        kt = kvt[hd * XA_DIM:(hd + 1) * XA_DIM, :]
        ms = jnp.mean(kt * kt, axis=0, keepdims=True)
        k_ref[hd * XA_DIM:(hd + 1) * XA_DIM, :] = kt * lax.rsqrt(ms + EPS) * kn
    v_ref[...] = kvt[XA_WIDTH:, :]


def _memory_kv(mem, mem_norm, w_kv_t, knorm_col):
    b, nm, d = mem.shape
    nl = w_kv_t.shape[0]
    out = jax.ShapeDtypeStruct((nl, b, XA_WIDTH, nm), F32)
    return pl.pallas_call(
        _memkv_body,
        out_shape=(out, out),
        grid=(nl, b),
        in_specs=[pl.BlockSpec((None, nm, d), lambda l, i: (i, 0, 0)),
                  pl.BlockSpec((None, 1, d), lambda l, i: (l, 0, 0)),
                  pl.BlockSpec((None, 2 * XA_WIDTH, d), lambda l, i: (l, 0, 0)),
                  pl.BlockSpec((None, XA_DIM, 1), lambda l, i: (l, 0, 0))],
        out_specs=(pl.BlockSpec((None, None, XA_WIDTH, nm), lambda l, i: (l, i, 0, 0)),
                   pl.BlockSpec((None, None, XA_WIDTH, nm), lambda l, i: (l, i, 0, 0))),
        compiler_params=_cparams(("parallel", "parallel")),
        name="memory_kv",
    )(mem, mem_norm, w_kv_t, knorm_col)


def _xattn_body(q_ref, mk_ref, mv_ref, qn_ref, bd_ref, o_ref, *, gs, tt, rg):
    q = _group_rms(q_ref[...], qn_ref[...], bd_ref[...], XA_DIM) * (XA_DIM ** -0.5)
    lane = _iota((1, XA_WIDTH), 1) // XA_DIM
    row = _iota((XA_WIDTH, 1), 0) // XA_DIM
    mkt = [mk_ref[g].astype(BF16) for g in range(gs)]
    mvt = [[jnp.where(row == hd, mv_ref[g], 0.0).astype(BF16) for hd in range(XA_HEADS)] for g in range(gs)]
    chains = [(g, r0, hd) for g in range(gs) for r0 in range(0, tt, rg) for hd in range(XA_HEADS)]
    qc = [jnp.where(lane == hd, q[g * tt + r0:g * tt + r0 + rg, :], 0.0).astype(BF16) for (g, r0, hd) in chains]
    s = [jnp.dot(q_, mkt[g], preferred_element_type=F32) for q_, (g, _, _) in zip(qc, chains)]
    p = [jnp.exp(s_ - jnp.max(s_, axis=-1, keepdims=True)) for s_ in s]
    inv = [1.0 / jnp.sum(p_, axis=-1, keepdims=True) for p_ in p]
    t = [lax.dot_general(p_.astype(BF16), mvt[g][hd], _NT, preferred_element_type=F32) * i_
         for p_, i_, (g, _, hd) in zip(p, inv, chains)]
    for c0 in range(0, len(chains), XA_HEADS):
        g, r0, _ = chains[c0]
        o_ref[g * tt + r0:g * tt + r0 + rg, :] = (t[c0] + t[c0 + 1]) + (t[c0 + 2] + t[c0 + 3])


def _mem_attend(pp, mk, mv, qn_t, bd64, *, groups, t, tt, gs, rg, col_blk):
    nt = t // tt
    assert gs == 1 or nt == 1
    return pl.pallas_call(
        functools.partial(_xattn_body, gs=gs, tt=tt, rg=rg),
        out_shape=jax.ShapeDtypeStruct((groups * t, XA_WIDTH), F32),
        grid=(groups // gs, nt),
        in_specs=[pl.BlockSpec((gs * tt, XA_WIDTH), lambda g, i: (g * nt + i, col_blk)),
                  pl.BlockSpec((gs, XA_WIDTH, N_MEM), lambda g, i: (g, 0, 0)),
                  pl.BlockSpec((gs, XA_WIDTH, N_MEM), lambda g, i: (g, 0, 0)),
                  pl.BlockSpec((1, XA_WIDTH), lambda g, i: (0, 0)),
                  pl.BlockSpec((256, 256), lambda g, i: (0, 0))],
        out_specs=pl.BlockSpec((gs * tt, XA_WIDTH), lambda g, i: (g * nt + i, 0)),
        compiler_params=_cparams(("parallel", "parallel")),
        name="mem_attend",
    )(pp, mk, mv, qn_t, bd64)


def _gdn_body(qkv_ref, z_ref, ba_ref, conv0_ref, s0_ref, cw_ref, hp_ref, nw_ref,
              o_ref, sout_ref, xbuf, cbuf, s_scr, *, r, c):
    j = pl.program_id(1)
    kw = N_HEADS128 * HEAD128

    @pl.when(j == 0)
    def _():
        xbuf[0:8, :] = conv0_ref[...]
        s_scr[...] = s0_ref[...]

    x = qkv_ref[...]
    xbuf[8:8 + r, :] = x
    cw = cw_ref[...]
    conv = x * cw[3:4, :]
    for i in range(1, 4):
        conv = conv + xbuf[8 - i:8 - i + r, :] * cw[3 - i:4 - i, :]
    cbuf[...] = _silu(conv)
    xbuf[0:8, :] = xbuf[r:r + 8, :]

    hp = hp_ref[...]
    incl = _tri(c, "incl")
    strict = _tri(c, "strict")
    cum_l = _b01(incl)
    cum_u = _b01(_tri(c, "upper_incl"))
    eye = jnp.where(_iota((c, c), 0) == _iota((c, c), 1), 1.0, 0.0)
    nw = nw_ref[...]
    heads = range(N_HEADS128)
    lo = [hd * HEAD128 for hd in heads]

    pre = []
    for ch in range(r // c):
        rows = slice(ch * c, (ch + 1) * c)
        ba = ba_ref[rows, :]
        beta_all = _sigmoid(ba)
        g_all = -jnp.exp(hp[0:1, :]) * jax.nn.softplus(ba + hp[1:2, :])
        gcum = _sel_l(cum_l, g_all)
        gcum_t = _sel_r(g_all, cum_u, _TN)
        eg_all = jnp.exp(gcum)
        q = [cbuf[rows, l:l + HEAD128] for l in lo]
        k = [cbuf[rows, kw + l:kw + l + HEAD128] for l in lo]
        v = [cbuf[rows, 2 * kw + l:2 * kw + l + HEAD128] for l in lo]
        q = [x_ * lax.rsqrt(jnp.sum(x_ * x_, axis=-1, keepdims=True) + EPS) * (HEAD128 ** -0.5) for x_ in q]
        k = [x_ * lax.rsqrt(jnp.sum(x_ * x_, axis=-1, keepdims=True) + EPS) for x_ in k]
        bcol = [beta_all[:, hd:hd + 1] for hd in heads]
        gc = [gcum[:, 8 + hd:9 + hd] for hd in heads]
        egc = [eg_all[:, 8 + hd:9 + hd] for hd in heads]
        gl = [gcum[c - 1:c, 8 + hd:9 + hd] for hd in heads]
        decay = [jnp.exp(jnp.where(incl, gc[hd] - gcum_t[8 + hd:9 + hd, :], -jnp.inf)) for hd in heads]
        kk = [_mm(k[hd], k[hd], _NT) for hd in heads]
        qk = [_mm(q[hd], k[hd], _NT) for hd in heads]
        a = [jnp.where(strict, bcol[hd] * kk[hd] * decay[hd], 0.0) for hd in heads]
        inv = [eye - a_ for a_ in a]
        pw = [_mm3(a_, a_) for a_ in a]
        inv = [inv[hd] + _mm3(inv[hd], pw[hd]) for hd in heads]
        n = 2
        while 2 * n < c:
            pw = [_mm3(p_, p_) for p_ in pw]
            inv = [inv[hd] + _mm3(inv[hd], pw[hd]) for hd in heads]
            n *= 2
        pre.append(dict(
            sol_v=[_mm3(inv[hd], v[hd] * bcol[hd]) for hd in heads],
            sol_k=[_mm3(inv[hd], k[hd] * (bcol[hd] * egc[hd])) for hd in heads],
            att=[qk[hd] * decay[hd] for hd in heads],
            qe=[q[hd] * egc[hd] for hd in heads],
            ke=[k[hd] * jnp.exp(gl[hd] - gc[hd]) for hd in heads],
            egl=[jnp.exp(gl[hd]) for hd in heads]))

    s = [s_scr[hd] for hd in heads]
    for ch, p_ in enumerate(pre):
        rows = slice(ch * c, (ch + 1) * c)
        u = [p_["sol_v"][hd] - _mm(p_["sol_k"][hd], s[hd]) for hd in heads]
        o = [_mm(p_["qe"][hd], s[hd]) + _mm(p_["att"][hd], u[hd]) for hd in heads]
        s = [s[hd] * p_["egl"][hd] + _mm(p_["ke"][hd], u[hd], _TN) for hd in heads]
        for hd in heads:
            o_ref[rows, lo[hd]:lo[hd] + HEAD128] = _rms_rows(o[hd], nw) * _silu(z_ref[rows, lo[hd]:lo[hd] + HEAD128])
    for hd in heads:
        s_scr[hd] = s[hd]

    @pl.when(j == pl.num_programs(1) - 1)
    def _():
        sout_ref[...] = s_scr[...]


def _gdn(pp, conv0, s0, cw, hp, nw, *, groups, t, r, c):
    nt = t // r
    cc = 3 * N_HEADS128 * HEAD128
    vw = N_HEADS128 * HEAD128
    return pl.pallas_call(
        functools.partial(_gdn_body, r=r, c=c),
        out_shape=(jax.ShapeDtypeStruct((groups * t, vw), F32),
                   jax.ShapeDtypeStruct((groups, N_HEADS128, HEAD128, HEAD128), F32)),
        grid=(groups, nt),
        in_specs=[pl.BlockSpec((r, cc), lambda g, i: (g * nt + i, 0)),
                  pl.BlockSpec((r, vw), lambda g, i: (g * nt + i, cc // vw)),
                  pl.BlockSpec((r, LANES), lambda g, i: (g * nt + i, (cc + vw + XA_WIDTH) // LANES)),
                  pl.BlockSpec((None, 8, cc), lambda g, i: (g, 0, 0)),
                  pl.BlockSpec((None, N_HEADS128, HEAD128, HEAD128), lambda g, i: (g, 0, 0, 0)),
                  pl.BlockSpec((4, cc), lambda g, i: (0, 0)),
                  pl.BlockSpec((2, LANES), lambda g, i: (0, 0)),
                  pl.BlockSpec((1, HEAD128), lambda g, i: (0, 0))],
        out_specs=(pl.BlockSpec((r, vw), lambda g, i: (g * nt + i, 0)),
                   pl.BlockSpec((None, N_HEADS128, HEAD128, HEAD128), lambda g, i: (g, 0, 0, 0))),
        scratch_shapes=[pltpu.VMEM((r + 8, cc), F32), pltpu.VMEM((r, cc), F32),
                        pltpu.VMEM((N_HEADS128, HEAD128, HEAD128), F32)],
        compiler_params=_cparams(("parallel", "arbitrary")),
        name="gdn",
    )(pp, pp, pp, conv0, s0, cw, hp, nw)


def _hgrn_body(q_ref, f_ref, i_ref, g_ref, s0_ref, lb_ref, nw_ref, o_ref, sout_ref, st_scr, *, r, c):
    j = pl.program_id(1)

    @pl.when(j == 0)
    def _():
        for hd in range(N_HEADS128):
            st_scr[hd] = s0_ref[hd].T

    nw = nw_ref[...]
    cum_m = _b01(_tri(c, "incl"))
    nsub = c // HG_SUB
    tri8 = _iota((HG_SUB, 1), 0)
    heads = range(N_HEADS128)
    lo = [hd * HEAD128 for hd in heads]
    lb = [lb_ref[:, l:l + HEAD128] for l in lo]
    log_lb = [jnp.log(x_) for x_ in lb]
    log_1m = [jnp.log1p(-x_) for x_ in lb]
    qe_all, upd_all, ebl_all, intra_all = [], [], [], []
    for ch in range(r // c):
        r0 = ch * c
        ff = [f_ref[r0:r0 + c, l:l + HEAD128] for l in lo]
        logf = [_logaddexp(log_lb[hd], log_1m[hd] + _log_sigmoid(ff[hd])) for hd in heads]
        k = [(1.0 - lb[hd]) * _sigmoid(-ff[hd]) for hd in heads]
        q = [_silu(q_ref[r0:r0 + c, l:l + HEAD128]) * (HEAD128 ** -0.5) for l in lo]
        v = [i_ref[r0:r0 + c, l:l + HEAD128] for l in lo]
        bc = [_sel_l(cum_m, x_) for x_ in logf]
        bl = [bc[hd][c - 1:c, :] for hd in heads]
        qe_all.append([q[hd] * jnp.exp(bc[hd]) for hd in heads])
        ebl_all.append([jnp.exp(bl[hd]) for hd in heads])
        upd_all.append([_mm(v[hd], k[hd] * jnp.exp(bl[hd] - bc[hd]), _TN) for hd in heads])
        blocks = [[None] for hd in heads]
        for sb in range(1, nsub):
            b0 = sb * HG_SUB
            for hd in heads:
                bref = bc[hd][b0 - 1:b0, :]
                qs = q[hd][b0:b0 + HG_SUB, :] * jnp.exp(bc[hd][b0:b0 + HG_SUB, :] - bref)
                ks = k[hd][0:b0, :] * jnp.exp(bref - bc[hd][0:b0, :])
                att = _mm(qs, ks, _NT)
                blocks[hd].append(_mm(att, v[hd][0:b0, :]))
        bc2 = [bc[hd] * LOG2E for hd in heads]
        for sb in range(nsub):
            b0 = sb * HG_SUB
            for hd in heads:
                qb, bcb = q[hd][b0:b0 + HG_SUB, :], bc2[hd][b0:b0 + HG_SUB, :]
                ob = blocks[hd][sb]
                for s_ in range(HG_SUB):
                    w = jnp.exp2(bcb - bc2[hd][b0 + s_:b0 + s_ + 1, :])
                    col = jnp.sum(qb * k[hd][b0 + s_:b0 + s_ + 1, :] * w, axis=-1, keepdims=True)
                    if s_ > 0:
                        col = jnp.where(tri8 >= s_, col, 0.0)
                    t_ = col * v[hd][b0 + s_:b0 + s_ + 1, :]
                    ob = t_ if ob is None else ob + t_
                blocks[hd][sb] = ob
        intra_all.append([blocks[hd][0] if nsub == 1 else jnp.concatenate(blocks[hd], axis=0) for hd in heads])
    st = [st_scr[hd] for hd in heads]
    for ch in range(r // c):
        r0 = ch * c
        o = [intra_all[ch][hd] + _mm(qe_all[ch][hd], st[hd], _NT) for hd in heads]
        st = [st[hd] * ebl_all[ch][hd] + upd_all[ch][hd] for hd in heads]
        for hd in heads:
            o_ref[r0:r0 + c, lo[hd]:lo[hd] + HEAD128] = _rms_rows(o[hd], nw) * _silu(g_ref[r0:r0 + c, lo[hd]:lo[hd] + HEAD128])
    for hd in heads:
        st_scr[hd] = st[hd]

    @pl.when(j == pl.num_programs(1) - 1)
    def _():
        for hd in range(N_HEADS128):
            sout_ref[hd] = st_scr[hd].T


def _hgrn(pp, s0, lb, nw, *, groups, t, r, c, row_off):
    nt = t // r
    off = row_off // r
    w = N_HEADS128 * HEAD128
    rows = lambda col: pl.BlockSpec((r, w), lambda g, i: (off + g * nt + i, col))
    return pl.pallas_call(
        functools.partial(_hgrn_body, r=r, c=c),
        out_shape=(jax.ShapeDtypeStruct((groups * t, w), F32),
                   jax.ShapeDtypeStruct((groups, N_HEADS128, HEAD128, HEAD128), F32)),
        grid=(groups, nt),
        in_specs=[rows(0), rows(1), rows(2), rows(3),
                  pl.BlockSpec((None, N_HEADS128, HEAD128, HEAD128), lambda g, i: (g, 0, 0, 0)),
                  pl.BlockSpec((1, w), lambda g, i: (0, 0)),
                  pl.BlockSpec((1, HEAD128), lambda g, i: (0, 0))],
        out_specs=(pl.BlockSpec((r, w), lambda g, i: (g * nt + i, 0)),
                   pl.BlockSpec((None, N_HEADS128, HEAD128, HEAD128), lambda g, i: (g, 0, 0, 0))),
        scratch_shapes=[pltpu.VMEM((N_HEADS128, HEAD128, HEAD128), F32)],
        compiler_params=_cparams(("parallel", "arbitrary")),
        name="hgrn2",
    )(pp, pp, pp, pp, s0, lb, nw)


def _fox_prep_body(q_ref, k_ref, fl_ref, qn_ref, kn_ref, fb_ref, bd_ref, qo_ref, ko_ref, lf_ref, fc_ref, carry):
    j = pl.program_id(1)

    @pl.when(j == 0)
    def _():
        carry[...] = jnp.zeros_like(carry)

    bd = bd_ref[...]
    qo_ref[...] = _group_rms(q_ref[...], qn_ref[...], bd, FOX_DIM)
    ko_ref[...] = _group_rms(k_ref[...], kn_ref[...], bd, FOX_DIM)
    lf = jax.nn.log_sigmoid(fl_ref[...] + fb_ref[...])
    lf_ref[...] = lf
    tr = lf.shape[0]
    fc = _sel_l(_b01(_tri(tr, "incl")), lf) + carry[...]
    fc_ref[...] = fc
    carry[...] = fc[tr - 1:tr, :]


def _fox_prep(pp, qn_t, kn_t, fb, bd64, *, groups, t, tr, row_off):
    nt = t // tr
    off = row_off // tr
    w = FOX_HEADS * FOX_DIM
    n = groups * t
    big = jax.ShapeDtypeStruct((n, w), F32)
    small = jax.ShapeDtypeStruct((n, LANES), F32)
    return pl.pallas_call(
        _fox_prep_body,
        out_shape=(big, big, small, small),
        grid=(groups, nt),
        in_specs=[pl.BlockSpec((tr, w), lambda g, i: (off + g * nt + i, 0)),
                  pl.BlockSpec((tr, w), lambda g, i: (off + g * nt + i, 1)),
                  pl.BlockSpec((tr, LANES), lambda g, i: (off + g * nt + i, (4 * w + XA_WIDTH) // LANES)),
                  pl.BlockSpec((1, w), lambda g, i: (0, 0)),
                  pl.BlockSpec((1, w), lambda g, i: (0, 0)),
                  pl.BlockSpec((1, LANES), lambda g, i: (0, 0)),
                  pl.BlockSpec((256, 256), lambda g, i: (0, 0))],
        out_specs=(pl.BlockSpec((tr, w), lambda g, i: (g * nt + i, 0)),
                   pl.BlockSpec((tr, w), lambda g, i: (g * nt + i, 0)),
                   pl.BlockSpec((tr, LANES), lambda g, i: (g * nt + i, 0)),
                   pl.BlockSpec((tr, LANES), lambda g, i: (g * nt + i, 0))),
        scratch_shapes=[pltpu.VMEM((1, LANES), F32)],
        compiler_params=_cparams(("parallel", "arbitrary")),
        name="fox_prep",
    )(pp, pp, pp, qn_t, kn_t, fb, bd64)


def _fox_prep_t_body(q_ref, k_ref, v_ref, fl_ref, qn_ref, kn_ref, fb_ref, bd_ref,
                     qo_ref, kt_ref, vt_ref, lft_ref, fc_ref, kb_ref, carry):
    j = pl.program_id(1)

    @pl.when(j == 0)
    def _():
        carry[...] = jnp.zeros_like(carry)

    bd = bd_ref[...]
    qo_ref[...] = _group_rms(q_ref[...], qn_ref[...], bd, FOX_DIM)
    kt_ref[...] = _group_rms(k_ref[...], kn_ref[...], bd, FOX_DIM).T
    vt_ref[...] = v_ref[...].T
    lf = jax.nn.log_sigmoid(fl_ref[...] + fb_ref[...])
    tr = lf.shape[0]
    fc = _sel_l(_b01(_tri(tr, "incl")), lf) + carry[...]
    fc_ref[...] = fc
    carry[...] = fc[tr - 1:tr, :]
    lft_ref[...] = lf.T[0:FOX_HEADS, :]
    rr = _iota((LANES, LANES), 0) % FOX_BIAS_ROWS
    first = 2 * (_iota((LANES, LANES), 0) // FOX_BIAS_ROWS)
    lanei = _iota((LANES, LANES), 1)
    kbias = jnp.where((_iota((LANES, 1), 0) % FOX_BIAS_ROWS) // 3 == 1, 1.0, 0.0)
    for pi, part in enumerate(_split3(fc * LOG2E)):
        pick = ((rr == pi) & (lanei == first)) | ((rr == 6 + pi) & (lanei == first + 1))
        kbias = kbias + lax.dot_general(jnp.where(pick, -1.0, 0.0).astype(BF16), part, _NT, preferred_element_type=F32)
    kb_ref[...] = kbias.astype(BF16)


def _fox_prep_t(pp, qn_t, kn_t, fb, bd64, *, groups, t, tr):
    nt = t // tr
    w = FOX_HEADS * FOX_DIM
    n = groups * t
    rows = lambda col: pl.BlockSpec((tr, w), lambda g, i: (g * nt + i, col))
    const = lambda g, i: (0, 0)
    tmaj = jax.ShapeDtypeStruct((groups, w, t), F32)
    hmaj = jax.ShapeDtypeStruct((groups, FOX_HEADS, t), F32)
    return pl.pallas_call(
        _fox_prep_t_body,
        out_shape=(jax.ShapeDtypeStruct((n, w), F32), tmaj, tmaj, hmaj, jax.ShapeDtypeStruct((n, LANES), F32),
                   jax.ShapeDtypeStruct((groups, LANES, t), BF16)),
        grid=(groups, nt),
        in_specs=[rows(0), rows(1), rows(2),
                  pl.BlockSpec((tr, LANES), lambda g, i: (g * nt + i, (4 * w + XA_WIDTH) // LANES)),
                  pl.BlockSpec((1, w), const), pl.BlockSpec((1, w), const), pl.BlockSpec((1, LANES), const),
                  pl.BlockSpec((256, 256), const)],
        out_specs=(pl.BlockSpec((tr, w), lambda g, i: (g * nt + i, 0)),
                   pl.BlockSpec((None, w, tr), lambda g, i: (g, 0, i)),
                   pl.BlockSpec((None, w, tr), lambda g, i: (g, 0, i)),
                   pl.BlockSpec((None, FOX_HEADS, tr), lambda g, i: (g, 0, i)),
                   pl.BlockSpec((tr, LANES), lambda g, i: (g * nt + i, 0)),
                   pl.BlockSpec((None, LANES, tr), lambda g, i: (g, 0, i))),
        scratch_shapes=[pltpu.VMEM((1, LANES), F32)],
        compiler_params=_cparams(("parallel", "arbitrary")),
        name="fox_prep_t",
    )(pp, pp, pp, pp, qn_t, kn_t, fb, bd64)


def _fox_flash_body(q_ref, k_ref, kb_ref, v_ref, og_ref, fq_ref, o_ref, m_scr, l_scr, acc_scr, *, tq):
    hp = pl.program_id(1)
    qi = pl.program_id(2)
    lane = _iota((1, LANES), 1)
    q = q_ref[...] * (FOX_DIM ** -0.5 * LOG2E)
    fq_parts = _split3(fq_ref[...] * LOG2E)
    brow = _iota((LANES, FOX_BIAS_ROWS), 0)
    bcol = _iota((LANES, FOX_BIAS_ROWS), 1)
    ones_at = _iota((1, FOX_BIAS_ROWS), 1) // 3
    qs = []
    for j in range(2):
        qb = jnp.where(ones_at == 2 * j, 1.0, 0.0)
        for pi, part in enumerate(fq_parts):
            pick = _b01((brow == 2 * hp + j) & (bcol == 3 + pi))
            qb = qb + jnp.dot(part, pick, preferred_element_type=F32)
        qm = jnp.where((lane // FOX_DIM) == j, q, 0.0).astype(BF16)
        qs.append(jnp.concatenate([qm, qb.astype(BF16)], axis=1))
        m_scr[j] = jnp.full((tq, LANES), -jnp.inf, F32)
        l_scr[j] = jnp.zeros((tq, LANES), F32)
        acc_scr[j] = jnp.zeros((tq, LANES), F32)

    def rep(x, n):
        return x if n == LANES else jnp.concatenate([x] * (n // LANES), axis=1)

    def block(ki, masked):
        k0 = pl.multiple_of(ki * tq, tq)
        kb = jnp.concatenate([k_ref[:, pl.ds(k0, tq)].astype(BF16),
                              kb_ref[:, pl.ds(k0, tq)]], axis=0)
        vb = v_ref[pl.ds(k0, tq), :].astype(BF16)
        chains = [(j, r0) for r0 in range(0, tq, FOX_RG) for j in range(2)]
        nk = [r0 + FOX_RG if masked else tq for (_, r0) in chains]
        s = [jnp.dot(qs[j][r0:r0 + FOX_RG, :], kb[:, :n_], preferred_element_type=F32)
             for (j, r0), n_ in zip(chains, nk)]
        if masked:
            s = [jnp.where(_iota((FOX_RG, n_), 1) <= _iota((FOX_RG, n_), 0) + r0, s_, -jnp.inf)
                 for (_, r0), n_, s_ in zip(chains, nk, s)]
        m_old = [m_scr[j, r0:r0 + FOX_RG, :] for (j, r0) in chains]
        m_new = [jnp.maximum(mo_, jnp.max(s_, axis=-1, keepdims=True)) for mo_, s_ in zip(m_old, s)]
        alpha = [jnp.exp2(mo_ - mn_) for mo_, mn_ in zip(m_old, m_new)]
        p = [jnp.exp2(s_ - rep(mn_, n_)) for s_, mn_, n_ in zip(s, m_new, nk)]
        pv = [jnp.dot(p_.astype(BF16), vb[:n_, :], preferred_element_type=F32) for p_, n_ in zip(p, nk)]
        for c_, (j, r0) in enumerate(chains):
            rows = slice(r0, r0 + FOX_RG)
            l_scr[j, rows, :] = alpha[c_] * l_scr[j, rows, :] + jnp.sum(p[c_], axis=-1, keepdims=True)
            acc_scr[j, rows, :] = alpha[c_] * acc_scr[j, rows, :] + pv[c_]
            m_scr[j, rows, :] = m_new[c_]

    def body(ki, carry):
        block(ki, False)
        return carry

    lax.fori_loop(0, qi, body, 0)
    block(qi, True)
    o0 = acc_scr[0] / l_scr[0]
    o1 = acc_scr[1] / l_scr[1]
    o_ref[...] = jnp.where(lane < FOX_DIM, o0, o1) * _sigmoid(og_ref[...])


def _fox_flash(qn, kt, kbias, pp, fc, *, groups, t, tq):
    nq = t // tq
    w = FOX_HEADS * FOX_DIM
    hp_n = w // LANES
    vblk = 2 * w // LANES
    gblk = 3 * w // LANES
    return pl.pallas_call(
        functools.partial(_fox_flash_body, tq=tq),
        out_shape=jax.ShapeDtypeStruct((groups * t, w), F32),
        grid=(groups, hp_n, nq),
        in_specs=[pl.BlockSpec((tq, LANES), lambda g, h, i: (g * nq + i, h)),
                  pl.BlockSpec((None, LANES, t), lambda g, h, i: (g, h, 0)),
                  pl.BlockSpec((None, FOX_BIAS_ROWS, t), lambda g, h, i: (g, h, 0)),
                  pl.BlockSpec((t, LANES), lambda g, h, i: (g, vblk + h)),
                  pl.BlockSpec((tq, LANES), lambda g, h, i: (g * nq + i, gblk + h)),
                  pl.BlockSpec((tq, LANES), lambda g, h, i: (g * nq + i, 0))],
        out_specs=pl.BlockSpec((tq, LANES), lambda g, h, i: (g * nq + i, h)),
        scratch_shapes=[pltpu.VMEM((2, tq, LANES), F32)] * 3,
        compiler_params=_cparams(("parallel", "parallel", "arbitrary")),
        name="fox_flash",
    )(qn, kt, kbias, pp, pp, fc)


def _fox_sample_body(pt_ref, *refs, gp, tnew):
    del pt_ref
    k_refs = refs[0:gp]
    v_refs = refs[gp:2 * gp]
    lf_refs = refs[2 * gp:3 * gp]
    q_ref, kn_ref, vn_ref, og_ref, lfn_ref, o_ref, q_scr, m_scr, l_scr, acc, carry = refs[3 * gp:]
    pg = pl.program_id(1)

    @pl.when(pg == 0)
    def _():
        qs = q_ref[...] * (FOX_DIM ** -0.5)
        for hd in range(FOX_HEADS):
            q_scr[hd] = qs[:, hd * FOX_DIM:(hd + 1) * FOX_DIM]
        m_scr[...] = jnp.full(m_scr.shape, -jnp.inf, F32)
        l_scr[...] = jnp.zeros_like(l_scr)
        acc[...] = jnp.zeros_like(acc)
        carry[...] = jnp.zeros_like(carry)

    fnew = _sel_l(_b01(_tri(tnew, "incl")), lfn_ref[...])

    heads = range(FOX_HEADS)

    def update(s, pv_fn):
        m_old = [m_scr[hd] for hd in heads]
        m_new = [jnp.maximum(m_old[hd], jnp.max(s[hd], axis=-1, keepdims=True)) for hd in heads]
        alpha = [jnp.exp(m_old[hd] - m_new[hd]) for hd in heads]
        p = [jnp.exp(s[hd] - m_new[hd]) for hd in heads]
        pv = [pv_fn(hd, p[hd].astype(BF16)) for hd in heads]
        for hd in heads:
            l_scr[hd] = alpha[hd] * l_scr[hd] + jnp.sum(p[hd], axis=-1, keepdims=True)
            acc[hd] = alpha[hd] * acc[hd] + pv[hd]
            m_scr[hd] = m_new[hd]

    later = _b01(_tri(PAGE, "strict"))
    rests = []
    run = carry[...]
    for j in range(gp):
        lft = lf_refs[j][...]
        rests.append(_sel_r(lft, later) + run)
        run = run + jnp.sum(lft, axis=-1, keepdims=True)
    carry[...] = run

    qh = [q_scr[hd].astype(BF16) for hd in heads]
    s = [jnp.concatenate(
        [jnp.dot(qh[hd], k_refs[j][hd].astype(BF16), preferred_element_type=F32) + rests[j][hd:hd + 1, :]
         for j in range(gp)], axis=1) + fnew[:, hd:hd + 1] for hd in heads]

    def pv_past(hd, p):
        parts = [lax.dot_general(p[:, j * PAGE:(j + 1) * PAGE], v_refs[j][hd].astype(BF16), _NT,
                                 preferred_element_type=F32) for j in range(gp)]
        while len(parts) > 1:
            parts = [parts[i] + parts[i + 1] for i in range(0, len(parts), 2)]
        return parts[0]

    update(s, pv_past)

    @pl.when(pg == pl.num_programs(1) - 1)
    def _():
        fnew_t = fnew.T
        causal = _tri(tnew, "incl")
        lo = [hd * FOX_DIM for hd in heads]
        qh = [q_scr[hd].astype(BF16) for hd in heads]
        s = [lax.dot_general(qh[hd], kn_ref[:, lo[hd]:lo[hd] + FOX_DIM].astype(BF16), _NT, preferred_element_type=F32)
             + fnew[:, hd:hd + 1] - fnew_t[hd:hd + 1, :] for hd in heads]
        s = [jnp.where(causal, s_, -jnp.inf) for s_ in s]
        update(s, lambda hd, p: jnp.dot(p, vn_ref[:, lo[hd]:lo[hd] + FOX_DIM].astype(BF16), preferred_element_type=F32))
        o_ref[...] = jnp.concatenate([acc[hd] / l_scr[hd] for hd in heads], axis=1) * _sigmoid(og_ref[...])


def _fox_sample(page_table, ckt, cvt, clft, qn, kn, pp, lfn, *, tnew, row_off, gp):
    ns, npages = page_table.shape
    w = FOX_HEADS * FOX_DIM
    ngrp = npages // gp
    off = row_off // tnew

    def page_map(j, nd):
        return lambda s, g, pt: (pt[s, npages - 1 - (g * gp + j)],) + (0,) * nd

    kv_specs = [pl.BlockSpec((None, FOX_HEADS, FOX_DIM, PAGE), page_map(j, 3)) for j in range(gp)]
    lf_specs = [pl.BlockSpec((None, FOX_HEADS, PAGE), page_map(j, 2)) for j in range(gp)]
    row = lambda s, g, pt: (s, 0)
    grid_spec = pltpu.PrefetchScalarGridSpec(
        num_scalar_prefetch=1,
        grid=(ns, ngrp),
        in_specs=kv_specs + kv_specs + lf_specs + [
            pl.BlockSpec((tnew, w), row),
            pl.BlockSpec((tnew, w), row),
            pl.BlockSpec((tnew, w), lambda s, g, pt: (off + s, 2)),
            pl.BlockSpec((tnew, w), lambda s, g, pt: (off + s, 3)),
            pl.BlockSpec((tnew, LANES), row)],
        out_specs=pl.BlockSpec((tnew, w), row),
        scratch_shapes=[pltpu.VMEM((FOX_HEADS, tnew, FOX_DIM), F32), pltpu.VMEM((FOX_HEADS, tnew, 1), F32),
                        pltpu.VMEM((FOX_HEADS, tnew, 1), F32), pltpu.VMEM((FOX_HEADS, tnew, FOX_DIM), F32),
                        pltpu.VMEM((FOX_HEADS, 1), F32)],
    )
    return pl.pallas_call(
        functools.partial(_fox_sample_body, gp=gp, tnew=tnew),
        out_shape=jax.ShapeDtypeStruct((ns * tnew, w), F32),
        grid_spec=grid_spec,
        compiler_params=_cparams(("parallel", "arbitrary")),
        name="fox_sample",
    )(page_table, *([ckt] * gp), *([cvt] * gp), *([clft] * gp), qn, kn, pp, pp, lfn)


def _cmlp_body(u_ref, v_ref, lg_ref, lbias_ref, ws_ref, bsb_ref, o_ref, *maybe_vo_ref):
    u = jax.nn.gelu(u_ref[...], approximate=True)
    z = jax.nn.gelu(v_ref[...], approximate=True)
    mu = jnp.mean(z, axis=-1, keepdims=True)
    zc = z - mu
    v = zc * lax.rsqrt(jnp.mean(zc * zc, axis=-1, keepdims=True) + EPS) * lg_ref[...] + lbias_ref[...]
    for vo_ref in maybe_vo_ref:
        vo_ref[...] = v
    tril = _tri(ws_ref.shape[1], "incl")
    gd = v.shape[1] // CM_GROUPS
    for g in range(CM_GROUPS):
        wm = jnp.where(tril, ws_ref[g], 0.0)
        mixed = _mm(wm, v[:, g * gd:(g + 1) * gd]) + bsb_ref[g]
        o_ref[:, g * gd:(g + 1) * gd] = u[:, g * gd:(g + 1) * gd] * mixed


def _chunk_mlp(pp, ln_g, ln_b, ws, bsb, *, nrows, row_off, emit_v):
    w = ln_g.shape[1]
    cr = ws.shape[1]
    off = row_off // cr
    out = jax.ShapeDtypeStruct((nrows, w), F32)
    const2 = lambda i: (0, 0)
    const3 = lambda i: (0, 0, 0)
    return pl.pallas_call(
        _cmlp_body,
        out_shape=(out, out) if emit_v else (out,),
        grid=(nrows // cr,),
        in_specs=[pl.BlockSpec((cr, w), lambda i: (off + i, 0)),
                  pl.BlockSpec((cr, w), lambda i: (off + i, 1)),
                  pl.BlockSpec((1, w), const2), pl.BlockSpec((1, w), const2),
                  pl.BlockSpec((CM_GROUPS, cr, cr), const3),
                  pl.BlockSpec((CM_GROUPS, cr, w // CM_GROUPS), const3)],
        out_specs=(pl.BlockSpec((cr, w), lambda i: (i, 0)),) * (2 if emit_v else 1),
        compiler_params=_cparams(("parallel",)),
        name="chunk_mlp",
    )(pp, pp, ln_g, ln_b, ws, bsb)


def _pad_cols(w, mult):
    pad = (-w.shape[1]) % mult
    return jnp.pad(w, ((0, 0), (0, pad))) if pad else w


def _lane_row(v, start=0):
    return jnp.zeros((1, LANES), F32).at[0, start:start + v.shape[0]].set(v.astype(F32))


def kernel(x_prompt, x_sample, mem_prompt, state_a_conv, state_a_ssm, state_b_ssm, cache_c_k, cache_c_v, cache_c_logf, cache_mem_k, cache_mem_v, page_table, norm_mix, w_out, norm_mlp, w_up, w_down, mem_norm, w_mem_kv, xa_qnorm, xa_knorm, w_in_a, a_conv_w, a_log, a_dt_bias, a_norm_w, w_in_b, hg_lb, b_norm_w, w_in_c, c_fbias, c_qnorm, c_knorm, w_in_d, d_ln_g, d_ln_b, d_ws, d_bs):
    bp, seq, d = x_prompt.shape
    ds, dseq, _ = x_sample.shape
    depth = norm_mix.shape[0]
    np_rows, ns_rows = bp * seq, ds * dseq
    tm_p = 512 if np_rows % 512 == 0 else 128
    tm_s = 256 if ns_rows % 256 == 0 else 8
    w1024 = N_HEADS128 * HEAD128
    cc = 3 * w1024

    h_p, h_s = x_prompt.reshape(np_rows, d), x_sample.reshape(ns_rows, d)
    bd64 = (jnp.arange(256)[:, None] // 64 == jnp.arange(256)[None, :] // 64).astype(BF16)

    mk_all, mv_all = _memory_kv(mem_prompt, mem_norm[:, None, :], jnp.swapaxes(w_mem_kv, 1, 2).astype(BF16),
                                xa_knorm[:, :, None])

    lb_w = jax.nn.softmax(hg_lb.astype(F32), axis=0)
    lower_bounds = jnp.cumsum(lb_w, axis=0) - lb_w[0]

    outs = {}
    for l in range(depth):
        kind, j = l % 4, l // 4
        if kind == 0:
            wi = w_in_a[j]
            w_packed = jnp.concatenate([wi[:, :cc + w1024], wi[:, cc + w1024 + 16:], _pad_cols(wi[:, cc + w1024:cc + w1024 + 16], LANES)], axis=1)
            xq_blk = (cc + w1024) // XA_WIDTH
        elif kind == 1:
            w_packed = w_in_b[j]
            xq_blk = 4 * w1024 // XA_WIDTH
        elif kind == 2:
            wi = w_in_c[j]
            w_packed = jnp.concatenate([wi[:, :4 * w1024], wi[:, 4 * w1024 + 16:], _pad_cols(wi[:, 4 * w1024:4 * w1024 + 16], LANES)], axis=1)
            xq_blk = 4 * w1024 // XA_WIDTH
        else:
            w_packed = w_in_d[j]
            xq_blk = 2 * w1024 // XA_WIDTH
        w_bf = w_packed.astype(BF16)
        g_mix = norm_mix[l][None, :]
        pp_p = _norm_proj(h_p, g_mix, w_bf, tm=tm_p)
        pp_s = _norm_proj(h_s, g_mix, w_bf, tm=tm_s)

        qn_t = jnp.tile(xa_qnorm[l], XA_HEADS)[None, :]
        xo_p = _mem_attend(pp_p, mk_all[l], mv_all[l], qn_t, bd64, groups=bp, t=seq, tt=512, gs=1, rg=128, col_blk=xq_blk)
        to_hdn = lambda a: jnp.transpose(a, (0, 2, 3, 1)).reshape(ds, XA_WIDTH, N_MEM)
        xo_s = _mem_attend(pp_s, to_hdn(cache_mem_k[l]), to_hdn(cache_mem_v[l]),
                           qn_t, bd64, groups=ds, t=dseq, tt=dseq, gs=4, rg=dseq, col_blk=xq_blk)

        if kind == 0:
            hp = jnp.concatenate([_lane_row(a_log[j], 8), _lane_row(a_dt_bias[j], 8)], axis=0)
            nw = a_norm_w[j][None, :]
            conv0_p = jnp.zeros((bp, 8, cc), F32)
            conv0_s = jnp.pad(state_a_conv[j], ((0, 0), (5, 0), (0, 0)))
            mo_p, st_p = _gdn(pp_p, conv0_p, jnp.zeros((bp, N_HEADS128, HEAD128, HEAD128), F32), a_conv_w[j], hp, nw,
                              groups=bp, t=seq, r=GDN_ROWS, c=GDN_CHUNK)
            mo_s, st_s = _gdn(pp_s, conv0_s, state_a_ssm[j], a_conv_w[j], hp, nw,
                              groups=ds, t=dseq, r=dseq, c=dseq)
            outs.setdefault("a_conv_p", []).append(pp_p.reshape(bp, seq, -1)[:, seq - 3:, :cc])
            outs.setdefault("a_conv_s", []).append(pp_s.reshape(ds, dseq, -1)[:, dseq - 3:, :cc])
            outs.setdefault("a_ssm_p", []).append(st_p)
            outs.setdefault("a_ssm_s", []).append(st_s)
        elif kind == 1:
            lb = lower_bounds[l][None, :]
            nw = b_norm_w[j][None, :]
            mo_p, st_p = _hgrn(pp_p, jnp.zeros((bp, N_HEADS128, HEAD128, HEAD128), F32), lb, nw,
                               groups=bp, t=seq, r=HG_ROWS, c=HG_CHUNK, row_off=0)
            mo_s, st_s = _hgrn(pp_s, state_b_ssm[j], lb, nw, groups=ds, t=dseq, r=dseq, c=dseq, row_off=0)
            outs.setdefault("b_ssm_p", []).append(st_p)
            outs.setdefault("b_ssm_s", []).append(st_s)
        elif kind == 2:
            qn_f = jnp.tile(c_qnorm[j], FOX_HEADS)[None, :]
            kn_f = jnp.tile(c_knorm[j], FOX_HEADS)[None, :]
            fb = _lane_row(c_fbias[j])
            q_p, kt_p, vt_p, lft_p, fc_p, kb_p = _fox_prep_t(pp_p, qn_f, kn_f, fb, bd64, groups=bp, t=seq, tr=256)
            q_s, k_s, lf_s, _ = _fox_prep(pp_s, qn_f, kn_f, fb, bd64, groups=ds, t=dseq, tr=dseq, row_off=0)
            mo_p = _fox_flash(q_p, kt_p, kb_p, pp_p, fc_p, groups=bp, t=seq, tq=FOX_TQ)
            pos_minor = lambda a: jnp.transpose(a, (0, 2, 3, 1))
            mo_s = _fox_sample(page_table, pos_minor(cache_c_k[j]), pos_minor(cache_c_v[j]),
                               jnp.swapaxes(cache_c_logf[j], 1, 2), q_s, k_s, pp_s, lf_s,
                               tnew=dseq, row_off=0, gp=PAGES_PER_STEP)
            time_major = lambda a: jnp.transpose(a.reshape(bp, FOX_HEADS, FOX_DIM, seq), (0, 3, 1, 2))
            outs.setdefault("c_k_p", []).append(time_major(kt_p))
            outs.setdefault("c_v_p", []).append(time_major(vt_p))
            outs.setdefault("c_lf_p", []).append(jnp.swapaxes(lft_p, 1, 2))
            outs.setdefault("c_k_s", []).append(k_s.reshape(ds, dseq, FOX_HEADS, FOX_DIM))
            outs.setdefault("c_v_s", []).append(pp_s[:, 2 * w1024:3 * w1024].reshape(ds, dseq, FOX_HEADS, FOX_DIM))
            outs.setdefault("c_lf_s", []).append(lf_s[:, :FOX_HEADS].reshape(ds, dseq, FOX_HEADS))
        else:
            gd = w1024 // CM_GROUPS
            lg, lbias = d_ln_g[j][None, :], d_ln_b[j][None, :]
            (mo_p,) = _chunk_mlp(pp_p, lg, lbias, d_ws[j], jnp.broadcast_to(d_bs[j][:, :, None], (CM_GROUPS, CM_CHUNK, gd)),
                                 nrows=np_rows, row_off=0, emit_v=False)
            mo_s, v_s = _chunk_mlp(pp_s, lg, lbias, d_ws[j][:, :dseq, :dseq],
                                   jnp.broadcast_to(d_bs[j][:, :dseq, None], (CM_GROUPS, dseq, gd)), nrows=ns_rows, row_off=0,
                                   emit_v=True)
            outs.setdefault("d_v_s", []).append(v_s.reshape(ds, dseq, w1024))

        wo = w_out[l].astype(BF16)
        mlp_w = (wo[:w1024], wo[w1024:], norm_mlp[l][None, :], w_up[l].astype(BF16), w_down[l].astype(BF16))
        h_p = _out_mlp(h_p, mo_p, xo_p, *mlp_w, tm=tm_p)
        h_s = _out_mlp(h_s, mo_s, xo_s, *mlp_w, tm=tm_s)

    st = lambda name: jnp.stack(outs[name])
    mem_out = lambda a: jnp.transpose(a.reshape(depth, bp, XA_HEADS, XA_DIM, N_MEM), (0, 1, 4, 2, 3))
    return (h_p.reshape(bp, seq, d), h_s.reshape(ds, dseq, d),
            st("a_conv_p"), st("a_conv_s"), st("a_ssm_p"), st("a_ssm_s"),
            st("b_ssm_p"), st("b_ssm_s"),
            st("c_k_p"), st("c_v_p"), st("c_lf_p"), st("c_k_s"), st("c_v_s"), st("c_lf_s"),
            st("d_v_s"), mem_out(mk_all), mem_out(mv_all))
```

```python
import functools
import math

import jax
import jax.numpy as jnp
from jax import lax
from jax.experimental import pallas as pl
from jax.experimental.pallas import tpu as pltpu

F32 = jnp.float32
BF16 = jnp.bfloat16
EPS = 1e-6
LOG2E = 1.4426950408889634

LANES = 128
SUBLANES = 8
VMEM_LIMIT = 56 * 1024 * 1024

XA_HEADS, XA_DIM, N_MEM = 4, 64, 256
XA_WIDTH = XA_HEADS * XA_DIM
HEAD128 = 128
N_HEADS128 = 8
FOX_DIM, FOX_HEADS = 64, 16
PAGE = 128
CM_CHUNK, CM_GROUPS = 128, 8
GDN_CHUNK, HG_CHUNK, HG_SUB = 64, 32, 8
GDN_ROWS = 512
HG_ROWS = 256
FOX_TQ = 512
FOX_RG = 128
FOX_BIAS_ROWS = 16
PAGES_PER_STEP = 16


def _cparams(sem):
    return pltpu.CompilerParams(dimension_semantics=sem, vmem_limit_bytes=VMEM_LIMIT)


def _iota(shape, dim):
    return lax.broadcasted_iota(jnp.int32, shape, dim)


def _split3(x):
    hi = x.astype(BF16)
    r = x - hi.astype(F32)
    mid = r.astype(BF16)
    lo = (r - mid.astype(F32)).astype(BF16)
    return hi, mid, lo


def _split2(x):
    hi = x.astype(BF16)
    lo = (x - hi.astype(F32)).astype(BF16)
    return hi, lo


_NN = (((1,), (0,)), ((), ()))
_NT = (((1,), (1,)), ((), ()))
_TN = (((0,), (0,)), ((), ()))


def _mm(a, b, dims=_NN):
    return lax.dot_general(a.astype(BF16), b.astype(BF16), dims, preferred_element_type=F32)


def _sel_l(m, x, dims=_NN):
    out = None
    for p in _split3(x):
        t = lax.dot_general(m, p, dims, preferred_element_type=F32)
        out = t if out is None else out + t
    return out


def _sel_r(x, m, dims=_NN):
    out = None
    for p in _split3(x):
        t = lax.dot_general(p, m, dims, preferred_element_type=F32)
        out = t if out is None else out + t
    return out


def _mm3(a, b):
    ah, al = _split2(a)
    bh, bl = _split2(b)
    d = functools.partial(jnp.dot, preferred_element_type=F32)
    return d(ah, bh) + d(ah, bl) + d(al, bh)


def _sigmoid(x):
    return jax.nn.sigmoid(x)


def _silu(x):
    return x * jax.nn.sigmoid(x)


def _log_sigmoid(x):
    return jnp.minimum(x, 0.0) - jnp.log1p(jnp.exp(-jnp.abs(x)))


def _logaddexp(a, b):
    return jnp.maximum(a, b) + jnp.log1p(jnp.exp(-jnp.abs(a - b)))


def _rms_rows(x, g):
    return x * lax.rsqrt(jnp.mean(x * x, axis=-1, keepdims=True) + EPS) * g


def _group_rms(x, gain, bd, gsz):
    w = x.shape[1]
    x2 = x * x
    parts = []
    for c in range(0, w, 256):
        parts.append(_sel_r(x2[:, c:c + 256], bd))
    ms = (parts[0] if len(parts) == 1 else jnp.concatenate(parts, axis=1)) * (1.0 / gsz)
    return x * lax.rsqrt(ms + EPS) * gain


def _tri(n, kind):
    r, c = _iota((n, n), 0), _iota((n, n), 1)
    if kind == "incl":
        return r >= c
    if kind == "strict":
        return r > c
    if kind == "upper_incl":
        return r <= c
    if kind == "upper_strict":
        return r < c
    raise ValueError(kind)


def _b01(mask):
    return jnp.where(mask, 1.0, 0.0).astype(BF16)


def _proj_body(x_ref, g_ref, w_ref, o_ref, *, tn):
    xn = _rms_rows(x_ref[...], g_ref[...]).astype(BF16)
    wp = w_ref.shape[1]
    for c in range(0, wp, tn):
        e = min(c + tn, wp)
        o_ref[:, c:e] = jnp.dot(xn, w_ref[:, c:e], preferred_element_type=F32)


def _norm_proj(h, g, w, *, tm, tn=512):
    n, d = h.shape
    wp = w.shape[1]
    return pl.pallas_call(
        functools.partial(_proj_body, tn=tn),
        out_shape=jax.ShapeDtypeStruct((n, wp), F32),
        grid=(n // tm,),
        in_specs=[pl.BlockSpec((tm, d), lambda i: (i, 0)),
                  pl.BlockSpec((1, d), lambda i: (0, 0)),
                  pl.BlockSpec((d, wp), lambda i: (0, 0), pipeline_mode=pl.Buffered(1))],
        out_specs=pl.BlockSpec((tm, wp), lambda i: (i, 0)),
        compiler_params=_cparams(("parallel",)),
        name="norm_proj",
    )(h, g, w)


def _out_mlp_body(h_ref, mo_ref, xo_ref, wo1_ref, wo2_ref, g_ref, wu_ref, wd_ref, o_ref, *, tf):
    h2 = (h_ref[...]
          + jnp.dot(mo_ref[...].astype(BF16), wo1_ref[...], preferred_element_type=F32)
          + jnp.dot(xo_ref[...].astype(BF16), wo2_ref[...], preferred_element_type=F32))
    xn = _rms_rows(h2, g_ref[...]).astype(BF16)
    acc = None
    dff = wu_ref.shape[1]
    for c in range(0, dff, tf):
        up = jnp.dot(xn, wu_ref[:, c:c + tf], preferred_element_type=F32)
        act = jnp.square(jnp.maximum(up, 0.0)).astype(BF16)
        t = jnp.dot(act, wd_ref[c:c + tf, :], preferred_element_type=F32)
        acc = t if acc is None else acc + t
    o_ref[...] = h2 + acc


def _out_mlp(h, mo, xo, wo1, wo2, g, wu, wd, *, tm, tf=512):
    n, d = h.shape
    dff = wu.shape[1]
    xw = xo.shape[1]
    const = lambda i: (0, 0)
    one = pl.Buffered(1)
    return pl.pallas_call(
        functools.partial(_out_mlp_body, tf=tf),
        out_shape=jax.ShapeDtypeStruct((n, d), F32),
        grid=(n // tm,),
        in_specs=[pl.BlockSpec((tm, d), lambda i: (i, 0)),
                  pl.BlockSpec((tm, d), lambda i: (i, 0)),
                  pl.BlockSpec((tm, xw), lambda i: (i, 0)),
                  pl.BlockSpec((d, d), const, pipeline_mode=one),
                  pl.BlockSpec((xw, d), const, pipeline_mode=one),
                  pl.BlockSpec((1, d), const),
                  pl.BlockSpec((d, dff), const, pipeline_mode=one),
                  pl.BlockSpec((dff, d), const, pipeline_mode=one)],
        out_specs=pl.BlockSpec((tm, d), lambda i: (i, 0)),
        compiler_params=_cparams(("parallel",)),
        name="out_mlp",
    )(h, mo, xo, wo1, wo2, g, wu, wd)


def _memkv_body(x_ref, g_ref, wt_ref, kn_ref, k_ref, v_ref):
    xn = _rms_rows(x_ref[...], g_ref[...]).astype(BF16)
    kvt = lax.dot_general(wt_ref[...], xn, _NT, preferred_element_type=F32)
    kn = kn_ref[...]
    for hd in range(XA_HEADS):
        kt = kvt[hd * XA_DIM:(hd + 1) * XA_DIM, :]
        ms = jnp.mean(kt * kt, axis=0, keepdims=True)
        k_ref[hd * XA_DIM:(hd + 1) * XA_DIM, :] = kt * lax.rsqrt(ms + EPS) * kn
    v_ref[...] = kvt[XA_WIDTH:, :]


def _memory_kv(mem, mem_norm, w_kv_t, knorm_col):
    b, nm, d = mem.shape
    nl = w_kv_t.shape[0]
    out = jax.ShapeDtypeStruct((nl, b, XA_WIDTH, nm), F32)
    return pl.pallas_call(
        _memkv_body,
        out_shape=(out, out),
        grid=(nl, b),
        in_specs=[pl.BlockSpec((None, nm, d), lambda l, i: (i, 0, 0)),
                  pl.BlockSpec((None, 1, d), lambda l, i: (l, 0, 0)),
                  pl.BlockSpec((None, 2 * XA_WIDTH, d), lambda l, i: (l, 0, 0)),
                  pl.BlockSpec((None, XA_DIM, 1), lambda l, i: (l, 0, 0))],
        out_specs=(pl.BlockSpec((None, None, XA_WIDTH, nm), lambda l, i: (l, i, 0, 0)),
                   pl.BlockSpec((None, None, XA_WIDTH, nm), lambda l, i: (l, i, 0, 0))),
        compiler_params=_cparams(("parallel", "parallel")),
        name="memory_kv",
    )(mem, mem_norm, w_kv_t, knorm_col)


def _xattn_body(q_ref, mk_ref, mv_ref, qn_ref, bd_ref, o_ref, *, gs, tt, rg):
    q = _group_rms(q_ref[...], qn_ref[...], bd_ref[...], XA_DIM) * (XA_DIM ** -0.5)
    lane = _iota((1, XA_WIDTH), 1) // XA_DIM
    row = _iota((XA_WIDTH, 1), 0) // XA_DIM
    mkt = [mk_ref[g].astype(BF16) for g in range(gs)]
    mvt = [[jnp.where(row == hd, mv_ref[g], 0.0).astype(BF16) for hd in range(XA_HEADS)] for g in range(gs)]
    chains = [(g, r0, hd) for g in range(gs) for r0 in range(0, tt, rg) for hd in range(XA_HEADS)]
    qc = [jnp.where(lane == hd, q[g * tt + r0:g * tt + r0 + rg, :], 0.0).astype(BF16) for (g, r0, hd) in chains]
    s = [jnp.dot(q_, mkt[g], preferred_element_type=F32) for q_, (g, _, _) in zip(qc, chains)]
    p = [jnp.exp(s_ - jnp.max(s_, axis=-1, keepdims=True)) for s_ in s]
    inv = [1.0 / jnp.sum(p_, axis=-1, keepdims=True) for p_ in p]
    t = [lax.dot_general(p_.astype(BF16), mvt[g][hd], _NT, preferred_element_type=F32) * i_
         for p_, i_, (g, _, hd) in zip(p, inv, chains)]
    for c0 in range(0, len(chains), XA_HEADS):
        g, r0, _ = chains[c0]
        o_ref[g * tt + r0:g * tt + r0 + rg, :] = (t[c0] + t[c0 + 1]) + (t[c0 + 2] + t[c0 + 3])


def _mem_attend(pp, mk, mv, qn_t, bd64, *, groups, t, tt, gs, rg, col_blk):
    nt = t // tt
    assert gs == 1 or nt == 1
    return pl.pallas_call(
        functools.partial(_xattn_body, gs=gs, tt=tt, rg=rg),
        out_shape=jax.ShapeDtypeStruct((groups * t, XA_WIDTH), F32),
        grid=(groups // gs, nt),
        in_specs=[pl.BlockSpec((gs * tt, XA_WIDTH), lambda g, i: (g * nt + i, col_blk)),
                  pl.BlockSpec((gs, XA_WIDTH, N_MEM), lambda g, i: (g, 0, 0)),
                  pl.BlockSpec((gs, XA_WIDTH, N_MEM), lambda g, i: (g, 0, 0)),
                  pl.BlockSpec((1, XA_WIDTH), lambda g, i: (0, 0)),
                  pl.BlockSpec((256, 256), lambda g, i: (0, 0))],
        out_specs=pl.BlockSpec((gs * tt, XA_WIDTH), lambda g, i: (g * nt + i, 0)),
        compiler_params=_cparams(("parallel", "parallel")),
        name="mem_attend",
    )(pp, mk, mv, qn_t, bd64)


def _gdn_body(qkv_ref, z_ref, ba_ref, conv0_ref, s0_ref, cw_ref, hp_ref, nw_ref,
              o_ref, sout_ref, xbuf, cbuf, s_scr, *, r, c):
    j = pl.program_id(1)
    kw = N_HEADS128 * HEAD128

    @pl.when(j == 0)
    def _():
        xbuf[0:8, :] = conv0_ref[...]
        s_scr[...] = s0_ref[...]

    x = qkv_ref[...]
    xbuf[8:8 + r, :] = x
    cw = cw_ref[...]
    conv = x * cw[3:4, :]
    for i in range(1, 4):
        conv = conv + xbuf[8 - i:8 - i + r, :] * cw[3 - i:4 - i, :]
    cbuf[...] = _silu(conv)
    xbuf[0:8, :] = xbuf[r:r + 8, :]

    hp = hp_ref[...]
    incl = _tri(c, "incl")
    strict = _tri(c, "strict")
    cum_l = _b01(incl)
    cum_u = _b01(_tri(c, "upper_incl"))
    eye = jnp.where(_iota((c, c), 0) == _iota((c, c), 1), 1.0, 0.0)
    nw = nw_ref[...]
    heads = range(N_HEADS128)
    lo = [hd * HEAD128 for hd in heads]

    pre = []
    for ch in range(r // c):
        rows = slice(ch * c, (ch + 1) * c)
        ba = ba_ref[rows, :]
        beta_all = _sigmoid(ba)
        g_all = -jnp.exp(hp[0:1, :]) * jax.nn.softplus(ba + hp[1:2, :])
        gcum = _sel_l(cum_l, g_all)
        gcum_t = _sel_r(g_all, cum_u, _TN)
        eg_all = jnp.exp(gcum)
        q = [cbuf[rows, l:l + HEAD128] for l in lo]
        k = [cbuf[rows, kw + l:kw + l + HEAD128] for l in lo]
        v = [cbuf[rows, 2 * kw + l:2 * kw + l + HEAD128] for l in lo]
        q = [x_ * lax.rsqrt(jnp.sum(x_ * x_, axis=-1, keepdims=True) + EPS) * (HEAD128 ** -0.5) for x_ in q]
        k = [x_ * lax.rsqrt(jnp.sum(x_ * x_, axis=-1, keepdims=True) + EPS) for x_ in k]
        bcol = [beta_all[:, hd:hd + 1] for hd in heads]
        gc = [gcum[:, 8 + hd:9 + hd] for hd in heads]
        egc = [eg_all[:, 8 + hd:9 + hd] for hd in heads]
        gl = [gcum[c - 1:c, 8 + hd:9 + hd] for hd in heads]
        decay = [jnp.exp(jnp.where(incl, gc[hd] - gcum_t[8 + hd:9 + hd, :], -jnp.inf)) for hd in heads]
        kk = [_mm(k[hd], k[hd], _NT) for hd in heads]
        qk = [_mm(q[hd], k[hd], _NT) for hd in heads]
        a = [jnp.where(strict, bcol[hd] * kk[hd] * decay[hd], 0.0) for hd in heads]
        inv = [eye - a_ for a_ in a]
        pw = [_mm3(a_, a_) for a_ in a]
        inv = [inv[hd] + _mm3(inv[hd], pw[hd]) for hd in heads]
        n = 2
        while 2 * n < c:
            pw = [_mm3(p_, p_) for p_ in pw]
            inv = [inv[hd] + _mm3(inv[hd], pw[hd]) for hd in heads]
            n *= 2
        pre.append(dict(
            sol_v=[_mm3(inv[hd], v[hd] * bcol[hd]) for hd in heads],
            sol_k=[_mm3(inv[hd], k[hd] * (bcol[hd] * egc[hd])) for hd in heads],
            att=[qk[hd] * decay[hd] for hd in heads],
            qe=[q[hd] * egc[hd] for hd in heads],
            ke=[k[hd] * jnp.exp(gl[hd] - gc[hd]) for hd in heads],
            egl=[jnp.exp(gl[hd]) for hd in heads]))

    s = [s_scr[hd] for hd in heads]
    for ch, p_ in enumerate(pre):
        rows = slice(ch * c, (ch + 1) * c)
        u = [p_["sol_v"][hd] - _mm(p_["sol_k"][hd], s[hd]) for hd in heads]
        o = [_mm(p_["qe"][hd], s[hd]) + _mm(p_["att"][hd], u[hd]) for hd in heads]
        s = [s[hd] * p_["egl"][hd] + _mm(p_["ke"][hd], u[hd], _TN) for hd in heads]
        for hd in heads:
            o_ref[rows, lo[hd]:lo[hd] + HEAD128] = _rms_rows(o[hd], nw) * _silu(z_ref[rows, lo[hd]:lo[hd] + HEAD128])
    for hd in heads:
        s_scr[hd] = s[hd]

    @pl.when(j == pl.num_programs(1) - 1)
    def _():
        sout_ref[...] = s_scr[...]


def _gdn(pp, conv0, s0, cw, hp, nw, *, groups, t, r, c):
    nt = t // r
    cc = 3 * N_HEADS128 * HEAD128
    vw = N_HEADS128 * HEAD128
    return pl.pallas_call(
        functools.partial(_gdn_body, r=r, c=c),
        out_shape=(jax.ShapeDtypeStruct((groups * t, vw), F32),
                   jax.ShapeDtypeStruct((groups, N_HEADS128, HEAD128, HEAD128), F32)),
        grid=(groups, nt),
        in_specs=[pl.BlockSpec((r, cc), lambda g, i: (g * nt + i, 0)),
                  pl.BlockSpec((r, vw), lambda g, i: (g * nt + i, cc // vw)),
                  pl.BlockSpec((r, LANES), lambda g, i: (g * nt + i, (cc + vw + XA_WIDTH) // LANES)),
                  pl.BlockSpec((None, 8, cc), lambda g, i: (g, 0, 0)),
                  pl.BlockSpec((None, N_HEADS128, HEAD128, HEAD128), lambda g, i: (g, 0, 0, 0)),
                  pl.BlockSpec((4, cc), lambda g, i: (0, 0)),
                  pl.BlockSpec((2, LANES), lambda g, i: (0, 0)),
                  pl.BlockSpec((1, HEAD128), lambda g, i: (0, 0))],
        out_specs=(pl.BlockSpec((r, vw), lambda g, i: (g * nt + i, 0)),
                   pl.BlockSpec((None, N_HEADS128, HEAD128, HEAD128), lambda g, i: (g, 0, 0, 0))),
        scratch_shapes=[pltpu.VMEM((r + 8, cc), F32), pltpu.VMEM((r, cc), F32),
                        pltpu.VMEM((N_HEADS128, HEAD128, HEAD128), F32)],
        compiler_params=_cparams(("parallel", "arbitrary")),
        name="gdn",
    )(pp, pp, pp, conv0, s0, cw, hp, nw)


def _hgrn_body(q_ref, f_ref, i_ref, g_ref, s0_ref, lb_ref, nw_ref, o_ref, sout_ref, st_scr, *, r, c):
    j = pl.program_id(1)

    @pl.when(j == 0)
    def _():
        for hd in range(N_HEADS128):
            st_scr[hd] = s0_ref[hd].T

    nw = nw_ref[...]
    cum_m = _b01(_tri(c, "incl"))
    nsub = c // HG_SUB
    tri8 = _iota((HG_SUB, 1), 0)
    heads = range(N_HEADS128)
    lo = [hd * HEAD128 for hd in heads]
    lb = [lb_ref[:, l:l + HEAD128] for l in lo]
    log_lb = [jnp.log(x_) for x_ in lb]
    log_1m = [jnp.log1p(-x_) for x_ in lb]
    qe_all, upd_all, ebl_all, intra_all = [], [], [], []
    for ch in range(r // c):
        r0 = ch * c
        ff = [f_ref[r0:r0 + c, l:l + HEAD128] for l in lo]
        logf = [_logaddexp(log_lb[hd], log_1m[hd] + _log_sigmoid(ff[hd])) for hd in heads]
        k = [(1.0 - lb[hd]) * _sigmoid(-ff[hd]) for hd in heads]
        q = [_silu(q_ref[r0:r0 + c, l:l + HEAD128]) * (HEAD128 ** -0.5) for l in lo]
        v = [i_ref[r0:r0 + c, l:l + HEAD128] for l in lo]
        bc = [_sel_l(cum_m, x_) for x_ in logf]
        bl = [bc[hd][c - 1:c, :] for hd in heads]
        qe_all.append([q[hd] * jnp.exp(bc[hd]) for hd in heads])
        ebl_all.append([jnp.exp(bl[hd]) for hd in heads])
        upd_all.append([_mm(v[hd], k[hd] * jnp.exp(bl[hd] - bc[hd]), _TN) for hd in heads])
        blocks = [[None] for hd in heads]
        for sb in range(1, nsub):
            b0 = sb * HG_SUB
            for hd in heads:
                bref = bc[hd][b0 - 1:b0, :]
                qs = q[hd][b0:b0 + HG_SUB, :] * jnp.exp(bc[hd][b0:b0 + HG_SUB, :] - bref)
                ks = k[hd][0:b0, :] * jnp.exp(bref - bc[hd][0:b0, :])
                att = _mm(qs, ks, _NT)
                blocks[hd].append(_mm(att, v[hd][0:b0, :]))
        bc2 = [bc[hd] * LOG2E for hd in heads]
        for sb in range(nsub):
            b0 = sb * HG_SUB
            for hd in heads:
                qb, bcb = q[hd][b0:b0 + HG_SUB, :], bc2[hd][b0:b0 + HG_SUB, :]
                ob = blocks[hd][sb]
                for s_ in range(HG_SUB):
                    w = jnp.exp2(bcb - bc2[hd][b0 + s_:b0 + s_ + 1, :])
                    col = jnp.sum(qb * k[hd][b0 + s_:b0 + s_ + 1, :] * w, axis=-1, keepdims=True)
                    if s_ > 0:
                        col = jnp.where(tri8 >= s_, col, 0.0)
                    t_ = col * v[hd][b0 + s_:b0 + s_ + 1, :]
                    ob = t_ if ob is None else ob + t_
                blocks[hd][sb] = ob
        intra_all.append([blocks[hd][0] if nsub == 1 else jnp.concatenate(blocks[hd], axis=0) for hd in heads])
    st = [st_scr[hd] for hd in heads]
    for ch in range(r // c):
        r0 = ch * c
        o = [intra_all[ch][hd] + _mm(qe_all[ch][hd], st[hd], _NT) for hd in heads]
        st = [st[hd] * ebl_all[ch][hd] + upd_all[ch][hd] for hd in heads]
        for hd in heads:
            o_ref[r0:r0 + c, lo[hd]:lo[hd] + HEAD128] = _rms_rows(o[hd], nw) * _silu(g_ref[r0:r0 + c, lo[hd]:lo[hd] + HEAD128])
    for hd in heads:
        st_scr[hd] = st[hd]

    @pl.when(j == pl.num_programs(1) - 1)
    def _():
        for hd in range(N_HEADS128):
            sout_ref[hd] = st_scr[hd].T


def _hgrn(pp, s0, lb, nw, *, groups, t, r, c, row_off):
    nt = t // r
    off = row_off // r
    w = N_HEADS128 * HEAD128
    rows = lambda col: pl.BlockSpec((r, w), lambda g, i: (off + g * nt + i, col))
    return pl.pallas_call(
        functools.partial(_hgrn_body, r=r, c=c),
        out_shape=(jax.ShapeDtypeStruct((groups * t, w), F32),
                   jax.ShapeDtypeStruct((groups, N_HEADS128, HEAD128, HEAD128), F32)),
        grid=(groups, nt),
        in_specs=[rows(0), rows(1), rows(2), rows(3),
                  pl.BlockSpec((None, N_HEADS128, HEAD128, HEAD128), lambda g, i: (g, 0, 0, 0)),
                  pl.BlockSpec((1, w), lambda g, i: (0, 0)),
                  pl.BlockSpec((1, HEAD128), lambda g, i: (0, 0))],
        out_specs=(pl.BlockSpec((r, w), lambda g, i: (g * nt + i, 0)),
                   pl.BlockSpec((None, N_HEADS128, HEAD128, HEAD128), lambda g, i: (g, 0, 0, 0))),
        scratch_shapes=[pltpu.VMEM((N_HEADS128, HEAD128, HEAD128), F32)],
        compiler_params=_cparams(("parallel", "arbitrary")),
        name="hgrn2",
    )(pp, pp, pp, pp, s0, lb, nw)


def _fox_prep_body(q_ref, k_ref, fl_ref, qn_ref, kn_ref, fb_ref, bd_ref, qo_ref, ko_ref, lf_ref, fc_ref, carry):
    j = pl.program_id(1)

    @pl.when(j == 0)
    def _():
        carry[...] = jnp.zeros_like(carry)

    bd = bd_ref[...]
    qo_ref[...] = _group_rms(q_ref[...], qn_ref[...], bd, FOX_DIM)
    ko_ref[...] = _group_rms(k_ref[...], kn_ref[...], bd, FOX_DIM)
    lf = jax.nn.log_sigmoid(fl_ref[...] + fb_ref[...])
    lf_ref[...] = lf
    tr = lf.shape[0]
    fc = _sel_l(_b01(_tri(tr, "incl")), lf) + carry[...]
    fc_ref[...] = fc
    carry[...] = fc[tr - 1:tr, :]


def _fox_prep(pp, qn_t, kn_t, fb, bd64, *, groups, t, tr, row_off):
    nt = t // tr
    off = row_off // tr
    w = FOX_HEADS * FOX_DIM
    n = groups * t
    big = jax.ShapeDtypeStruct((n, w), F32)
    small = jax.ShapeDtypeStruct((n, LANES), F32)
    return pl.pallas_call(
        _fox_prep_body,
        out_shape=(big, big, small, small),
        grid=(groups, nt),
        in_specs=[pl.BlockSpec((tr, w), lambda g, i: (off + g * nt + i, 0)),
                  pl.BlockSpec((tr, w), lambda g, i: (off + g * nt + i, 1)),
                  pl.BlockSpec((tr, LANES), lambda g, i: (off + g * nt + i, (4 * w + XA_WIDTH) // LANES)),
                  pl.BlockSpec((1, w), lambda g, i: (0, 0)),
                  pl.BlockSpec((1, w), lambda g, i: (0, 0)),
                  pl.BlockSpec((1, LANES), lambda g, i: (0, 0)),
                  pl.BlockSpec((256, 256), lambda g, i: (0, 0))],
        out_specs=(pl.BlockSpec((tr, w), lambda g, i: (g * nt + i, 0)),
                   pl.BlockSpec((tr, w), lambda g, i: (g * nt + i, 0)),
                   pl.BlockSpec((tr, LANES), lambda g, i: (g * nt + i, 0)),
                   pl.BlockSpec((tr, LANES), lambda g, i: (g * nt + i, 0))),
        scratch_shapes=[pltpu.VMEM((1, LANES), F32)],
        compiler_params=_cparams(("parallel", "arbitrary")),
        name="fox_prep",
    )(pp, pp, pp, qn_t, kn_t, fb, bd64)


def _fox_prep_t_body(q_ref, k_ref, v_ref, fl_ref, qn_ref, kn_ref, fb_ref, bd_ref,
                     qo_ref, kt_ref, vt_ref, lft_ref, fc_ref, kb_ref, carry):
    j = pl.program_id(1)

    @pl.when(j == 0)
    def _():
        carry[...] = jnp.zeros_like(carry)

    bd = bd_ref[...]
    qo_ref[...] = _group_rms(q_ref[...], qn_ref[...], bd, FOX_DIM)
    kt_ref[...] = _group_rms(k_ref[...], kn_ref[...], bd, FOX_DIM).T
    vt_ref[...] = v_ref[...].T
    lf = jax.nn.log_sigmoid(fl_ref[...] + fb_ref[...])
    tr = lf.shape[0]
    fc = _sel_l(_b01(_tri(tr, "incl")), lf) + carry[...]
    fc_ref[...] = fc
    carry[...] = fc[tr - 1:tr, :]
    lft_ref[...] = lf.T[0:FOX_HEADS, :]
    rr = _iota((LANES, LANES), 0) % FOX_BIAS_ROWS
    first = 2 * (_iota((LANES, LANES), 0) // FOX_BIAS_ROWS)
    lanei = _iota((LANES, LANES), 1)
    kbias = jnp.where((_iota((LANES, 1), 0) % FOX_BIAS_ROWS) // 3 == 1, 1.0, 0.0)
    for pi, part in enumerate(_split3(fc * LOG2E)):
        pick = ((rr == pi) & (lanei == first)) | ((rr == 6 + pi) & (lanei == first + 1))
        kbias = kbias + lax.dot_general(jnp.where(pick, -1.0, 0.0).astype(BF16), part, _NT, preferred_element_type=F32)
    kb_ref[...] = kbias.astype(BF16)


def _fox_prep_t(pp, qn_t, kn_t, fb, bd64, *, groups, t, tr):
    nt = t // tr
    w = FOX_HEADS * FOX_DIM
    n = groups * t
    rows = lambda col: pl.BlockSpec((tr, w), lambda g, i: (g * nt + i, col))
    const = lambda g, i: (0, 0)
    tmaj = jax.ShapeDtypeStruct((groups, w, t), F32)
    hmaj = jax.ShapeDtypeStruct((groups, FOX_HEADS, t), F32)
    return pl.pallas_call(
        _fox_prep_t_body,
        out_shape=(jax.ShapeDtypeStruct((n, w), F32), tmaj, tmaj, hmaj, jax.ShapeDtypeStruct((n, LANES), F32),
                   jax.ShapeDtypeStruct((groups, LANES, t), BF16)),
        grid=(groups, nt),
        in_specs=[rows(0), rows(1), rows(2),
                  pl.BlockSpec((tr, LANES), lambda g, i: (g * nt + i, (4 * w + XA_WIDTH) // LANES)),
                  pl.BlockSpec((1, w), const), pl.BlockSpec((1, w), const), pl.BlockSpec((1, LANES), const),
                  pl.BlockSpec((256, 256), const)],
        out_specs=(pl.BlockSpec((tr, w), lambda g, i: (g * nt + i, 0)),
                   pl.BlockSpec((None, w, tr), lambda g, i: (g, 0, i)),
                   pl.BlockSpec((None, w, tr), lambda g, i: (g, 0, i)),
                   pl.BlockSpec((None, FOX_HEADS, tr), lambda g, i: (g, 0, i)),
                   pl.BlockSpec((tr, LANES), lambda g, i: (g * nt + i, 0)),
                   pl.BlockSpec((None, LANES, tr), lambda g, i: (g, 0, i))),
        scratch_shapes=[pltpu.VMEM((1, LANES), F32)],
        compiler_params=_cparams(("parallel", "arbitrary")),
        name="fox_prep_t",
    )(pp, pp, pp, pp, qn_t, kn_t, fb, bd64)


def _fox_flash_body(q_ref, k_ref, kb_ref, v_ref, og_ref, fq_ref, o_ref, m_scr, l_scr, acc_scr, *, tq):
    hp = pl.program_id(1)
    qi = pl.program_id(2)
    lane = _iota((1, LANES), 1)
    q = q_ref[...] * (FOX_DIM ** -0.5 * LOG2E)
    fq_parts = _split3(fq_ref[...] * LOG2E)
    brow = _iota((LANES, FOX_BIAS_ROWS), 0)
    bcol = _iota((LANES, FOX_BIAS_ROWS), 1)
    ones_at = _iota((1, FOX_BIAS_ROWS), 1) // 3
    qs = []
    for j in range(2):
        qb = jnp.where(ones_at == 2 * j, 1.0, 0.0)
        for pi, part in enumerate(fq_parts):
            pick = _b01((brow == 2 * hp + j) & (bcol == 3 + pi))
            qb = qb + jnp.dot(part, pick, preferred_element_type=F32)
        qm = jnp.where((lane // FOX_DIM) == j, q, 0.0).astype(BF16)
        qs.append(jnp.concatenate([qm, qb.astype(BF16)], axis=1))
        m_scr[j] = jnp.full((tq, LANES), -jnp.inf, F32)
        l_scr[j] = jnp.zeros((tq, LANES), F32)
        acc_scr[j] = jnp.zeros((tq, LANES), F32)

    def rep(x, n):
        return x if n == LANES else jnp.concatenate([x] * (n // LANES), axis=1)

    def block(ki, masked):
        k0 = pl.multiple_of(ki * tq, tq)
        kb = jnp.concatenate([k_ref[:, pl.ds(k0, tq)].astype(BF16),
                              kb_ref[:, pl.ds(k0, tq)]], axis=0)
        vb = v_ref[pl.ds(k0, tq), :].astype(BF16)
        chains = [(j, r0) for r0 in range(0, tq, FOX_RG) for j in range(2)]
        nk = [r0 + FOX_RG if masked else tq for (_, r0) in chains]
        s = [jnp.dot(qs[j][r0:r0 + FOX_RG, :], kb[:, :n_], preferred_element_type=F32)
             for (j, r0), n_ in zip(chains, nk)]
        if masked:
            s = [jnp.where(_iota((FOX_RG, n_), 1) <= _iota((FOX_RG, n_), 0) + r0, s_, -jnp.inf)
                 for (_, r0), n_, s_ in zip(chains, nk, s)]
        m_old = [m_scr[j, r0:r0 + FOX_RG, :] for (j, r0) in chains]
        m_new = [jnp.maximum(mo_, jnp.max(s_, axis=-1, keepdims=True)) for mo_, s_ in zip(m_old, s)]
        alpha = [jnp.exp2(mo_ - mn_) for mo_, mn_ in zip(m_old, m_new)]
        p = [jnp.exp2(s_ - rep(mn_, n_)) for s_, mn_, n_ in zip(s, m_new, nk)]
        pv = [jnp.dot(p_.astype(BF16), vb[:n_, :], preferred_element_type=F32) for p_, n_ in zip(p, nk)]
        for c_, (j, r0) in enumerate(chains):
            rows = slice(r0, r0 + FOX_RG)
            l_scr[j, rows, :] = alpha[c_] * l_scr[j, rows, :] + jnp.sum(p[c_], axis=-1, keepdims=True)
            acc_scr[j, rows, :] = alpha[c_] * acc_scr[j, rows, :] + pv[c_]
            m_scr[j, rows, :] = m_new[c_]

    def body(ki, carry):
        block(ki, False)
        return carry

    lax.fori_loop(0, qi, body, 0)
    block(qi, True)
    o0 = acc_scr[0] / l_scr[0]
    o1 = acc_scr[1] / l_scr[1]
    o_ref[...] = jnp.where(lane < FOX_DIM, o0, o1) * _sigmoid(og_ref[...])


def _fox_flash(qn, kt, kbias, pp, fc, *, groups, t, tq):
    nq = t // tq
    w = FOX_HEADS * FOX_DIM
    hp_n = w // LANES
    vblk = 2 * w // LANES
    gblk = 3 * w // LANES
    return pl.pallas_call(
        functools.partial(_fox_flash_body, tq=tq),
        out_shape=jax.ShapeDtypeStruct((groups * t, w), F32),
        grid=(groups, hp_n, nq),
        in_specs=[pl.BlockSpec((tq, LANES), lambda g, h, i: (g * nq + i, h)),
                  pl.BlockSpec((None, LANES, t), lambda g, h, i: (g, h, 0)),
                  pl.BlockSpec((None, FOX_BIAS_ROWS, t), lambda g, h, i: (g, h, 0)),
                  pl.BlockSpec((t, LANES), lambda g, h, i: (g, vblk + h)),
                  pl.BlockSpec((tq, LANES), lambda g, h, i: (g * nq + i, gblk + h)),
                  pl.BlockSpec((tq, LANES), lambda g, h, i: (g * nq + i, 0))],
        out_specs=pl.BlockSpec((tq, LANES), lambda g, h, i: (g * nq + i, h)),
        scratch_shapes=[pltpu.VMEM((2, tq, LANES), F32)] * 3,
        compiler_params=_cparams(("parallel", "parallel", "arbitrary")),
        name="fox_flash",
    )(qn, kt, kbias, pp, pp, fc)


def _fox_sample_body(pt_ref, *refs, gp, tnew):
    del pt_ref
    k_refs = refs[0:gp]
    v_refs = refs[gp:2 * gp]
    lf_refs = refs[2 * gp:3 * gp]
    q_ref, kn_ref, vn_ref, og_ref, lfn_ref, o_ref, q_scr, m_scr, l_scr, acc, carry = refs[3 * gp:]
    pg = pl.program_id(1)

    @pl.when(pg == 0)
    def _():
        qs = q_ref[...] * (FOX_DIM ** -0.5)
        for hd in range(FOX_HEADS):
            q_scr[hd] = qs[:, hd * FOX_DIM:(hd + 1) * FOX_DIM]
        m_scr[...] = jnp.full(m_scr.shape, -jnp.inf, F32)
        l_scr[...] = jnp.zeros_like(l_scr)
        acc[...] = jnp.zeros_like(acc)
        carry[...] = jnp.zeros_like(carry)

    fnew = _sel_l(_b01(_tri(tnew, "incl")), lfn_ref[...])

    heads = range(FOX_HEADS)

    def update(s, pv_fn):
        m_old = [m_scr[hd] for hd in heads]
        m_new = [jnp.maximum(m_old[hd], jnp.max(s[hd], axis=-1, keepdims=True)) for hd in heads]
        alpha = [jnp.exp(m_old[hd] - m_new[hd]) for hd in heads]
        p = [jnp.exp(s[hd] - m_new[hd]) for hd in heads]
        pv = [pv_fn(hd, p[hd].astype(BF16)) for hd in heads]
        for hd in heads:
            l_scr[hd] = alpha[hd] * l_scr[hd] + jnp.sum(p[hd], axis=-1, keepdims=True)
            acc[hd] = alpha[hd] * acc[hd] + pv[hd]
            m_scr[hd] = m_new[hd]

    later = _b01(_tri(PAGE, "strict"))
    rests = []
    run = carry[...]
    for j in range(gp):
        lft = lf_refs[j][...]
        rests.append(_sel_r(lft, later) + run)
        run = run + jnp.sum(lft, axis=-1, keepdims=True)
    carry[...] = run

    qh = [q_scr[hd].astype(BF16) for hd in heads]
    s = [jnp.concatenate(
        [jnp.dot(qh[hd], k_refs[j][hd].astype(BF16), preferred_element_type=F32) + rests[j][hd:hd + 1, :]
         for j in range(gp)], axis=1) + fnew[:, hd:hd + 1] for hd in heads]

    def pv_past(hd, p):
        parts = [lax.dot_general(p[:, j * PAGE:(j + 1) * PAGE], v_refs[j][hd].astype(BF16), _NT,
                                 preferred_element_type=F32) for j in range(gp)]
        while len(parts) > 1:
            parts = [parts[i] + parts[i + 1] for i in range(0, len(parts), 2)]
        return parts[0]

    update(s, pv_past)

    @pl.when(pg == pl.num_programs(1) - 1)
    def _():
        fnew_t = fnew.T
        causal = _tri(tnew, "incl")
        lo = [hd * FOX_DIM for hd in heads]
        qh = [q_scr[hd].astype(BF16) for hd in heads]
        s = [lax.dot_general(qh[hd], kn_ref[:, lo[hd]:lo[hd] + FOX_DIM].astype(BF16), _NT, preferred_element_type=F32)
             + fnew[:, hd:hd + 1] - fnew_t[hd:hd + 1, :] for hd in heads]
        s = [jnp.where(causal, s_, -jnp.inf) for s_ in s]
        update(s, lambda hd, p: jnp.dot(p, vn_ref[:, lo[hd]:lo[hd] + FOX_DIM].astype(BF16), preferred_element_type=F32))
        o_ref[...] = jnp.concatenate([acc[hd] / l_scr[hd] for hd in heads], axis=1) * _sigmoid(og_ref[...])


def _fox_sample(page_table, ckt, cvt, clft, qn, kn, pp, lfn, *, tnew, row_off, gp):
    ns, npages = page_table.shape
    w = FOX_HEADS * FOX_DIM
    ngrp = npages // gp
    off = row_off // tnew

    def page_map(j, nd):
        return lambda s, g, pt: (pt[s, npages - 1 - (g * gp + j)],) + (0,) * nd

    kv_specs = [pl.BlockSpec((None, FOX_HEADS, FOX_DIM, PAGE), page_map(j, 3)) for j in range(gp)]
    lf_specs = [pl.BlockSpec((None, FOX_HEADS, PAGE), page_map(j, 2)) for j in range(gp)]
    row = lambda s, g, pt: (s, 0)
    grid_spec = pltpu.PrefetchScalarGridSpec(
        num_scalar_prefetch=1,
        grid=(ns, ngrp),
        in_specs=kv_specs + kv_specs + lf_specs + [
            pl.BlockSpec((tnew, w), row),
            pl.BlockSpec((tnew, w), row),
            pl.BlockSpec((tnew, w), lambda s, g, pt: (off + s, 2)),
            pl.BlockSpec((tnew, w), lambda s, g, pt: (off + s, 3)),
            pl.BlockSpec((tnew, LANES), row)],
        out_specs=pl.BlockSpec((tnew, w), row),
        scratch_shapes=[pltpu.VMEM((FOX_HEADS, tnew, FOX_DIM), F32), pltpu.VMEM((FOX_HEADS, tnew, 1), F32),
                        pltpu.VMEM((FOX_HEADS, tnew, 1), F32), pltpu.VMEM((FOX_HEADS, tnew, FOX_DIM), F32),
                        pltpu.VMEM((FOX_HEADS, 1), F32)],
    )
    return pl.pallas_call(
        functools.partial(_fox_sample_body, gp=gp, tnew=tnew),
        out_shape=jax.ShapeDtypeStruct((ns * tnew, w), F32),
        grid_spec=grid_spec,
        compiler_params=_cparams(("parallel", "arbitrary")),
        name="fox_sample",
    )(page_table, *([ckt] * gp), *([cvt] * gp), *([clft] * gp), qn, kn, pp, pp, lfn)


def _cmlp_body(u_ref, v_ref, lg_ref, lbias_ref, ws_ref, bsb_ref, o_ref, *maybe_vo_ref):
    u = jax.nn.gelu(u_ref[...], approximate=True)
    z = jax.nn.gelu(v_ref[...], approximate=True)
    mu = jnp.mean(z, axis=-1, keepdims=True)
    zc = z - mu
    v = zc * lax.rsqrt(jnp.mean(zc * zc, axis=-1, keepdims=True) + EPS) * lg_ref[...] + lbias_ref[...]
    for vo_ref in maybe_vo_ref:
        vo_ref[...] = v
    tril = _tri(ws_ref.shape[1], "incl")
    gd = v.shape[1] // CM_GROUPS
    for g in range(CM_GROUPS):
        wm = jnp.where(tril, ws_ref[g], 0.0)
        mixed = _mm(wm, v[:, g * gd:(g + 1) * gd]) + bsb_ref[g]
        o_ref[:, g * gd:(g + 1) * gd] = u[:, g * gd:(g + 1) * gd] * mixed


def _chunk_mlp(pp, ln_g, ln_b, ws, bsb, *, nrows, row_off, emit_v):
    w = ln_g.shape[1]
    cr = ws.shape[1]
    off = row_off // cr
    out = jax.ShapeDtypeStruct((nrows, w), F32)
    const2 = lambda i: (0, 0)
    const3 = lambda i: (0, 0, 0)
    return pl.pallas_call(
        _cmlp_body,
        out_shape=(out, out) if emit_v else (out,),
        grid=(nrows // cr,),
        in_specs=[pl.BlockSpec((cr, w), lambda i: (off + i, 0)),
                  pl.BlockSpec((cr, w), lambda i: (off + i, 1)),
                  pl.BlockSpec((1, w), const2), pl.BlockSpec((1, w), const2),
                  pl.BlockSpec((CM_GROUPS, cr, cr), const3),
                  pl.BlockSpec((CM_GROUPS, cr, w // CM_GROUPS), const3)],
        out_specs=(pl.BlockSpec((cr, w), lambda i: (i, 0)),) * (2 if emit_v else 1),
        compiler_params=_cparams(("parallel",)),
        name="chunk_mlp",
    )(pp, pp, ln_g, ln_b, ws, bsb)


def _pad_cols(w, mult):
    pad = (-w.shape[1]) % mult
    return jnp.pad(w, ((0, 0), (0, pad))) if pad else w


def _lane_row(v, start=0):
    return jnp.zeros((1, LANES), F32).at[0, start:start + v.shape[0]].set(v.astype(F32))


def kernel(x_prompt, x_sample, mem_prompt, state_a_conv, state_a_ssm, state_b_ssm, cache_c_k, cache_c_v, cache_c_logf, cache_mem_k, cache_mem_v, page_table, norm_mix, w_out, norm_mlp, w_up, w_down, mem_norm, w_mem_kv, xa_qnorm, xa_knorm, w_in_a, a_conv_w, a_log, a_dt_bias, a_norm_w, w_in_b, hg_lb, b_norm_w, w_in_c, c_fbias, c_qnorm, c_knorm, w_in_d, d_ln_g, d_ln_b, d_ws, d_bs):
    bp, seq, d = x_prompt.shape
    ds, dseq, _ = x_sample.shape
    depth = norm_mix.shape[0]
    np_rows, ns_rows = bp * seq, ds * dseq
    tm_p = 512 if np_rows % 512 == 0 else 128
    tm_s = 256 if ns_rows % 256 == 0 else 8
    w1024 = N_HEADS128 * HEAD128
    cc = 3 * w1024

    h_p, h_s = x_prompt.reshape(np_rows, d), x_sample.reshape(ns_rows, d)
    bd64 = (jnp.arange(256)[:, None] // 64 == jnp.arange(256)[None, :] // 64).astype(BF16)

    mk_all, mv_all = _memory_kv(mem_prompt, mem_norm[:, None, :], jnp.swapaxes(w_mem_kv, 1, 2).astype(BF16),
                                xa_knorm[:, :, None])

    lb_w = jax.nn.softmax(hg_lb.astype(F32), axis=0)
    lower_bounds = jnp.cumsum(lb_w, axis=0) - lb_w[0]

    outs = {}
    for l in range(depth):
        kind, j = l % 4, l // 4
        if kind == 0:
            wi = w_in_a[j]
            w_packed = jnp.concatenate([wi[:, :cc + w1024], wi[:, cc + w1024 + 16:], _pad_cols(wi[:, cc + w1024:cc + w1024 + 16], LANES)], axis=1)
            xq_blk = (cc + w1024) // XA_WIDTH
        elif kind == 1:
            w_packed = w_in_b[j]
            xq_blk = 4 * w1024 // XA_WIDTH
        elif kind == 2:
            wi = w_in_c[j]
            w_packed = jnp.concatenate([wi[:, :4 * w1024], wi[:, 4 * w1024 + 16:], _pad_cols(wi[:, 4 * w1024:4 * w1024 + 16], LANES)], axis=1)
            xq_blk = 4 * w1024 // XA_WIDTH
        else:
            w_packed = w_in_d[j]
            xq_blk = 2 * w1024 // XA_WIDTH
        w_bf = w_packed.astype(BF16)
        g_mix = norm_mix[l][None, :]
        pp_p = _norm_proj(h_p, g_mix, w_bf, tm=tm_p)
        pp_s = _norm_proj(h_s, g_mix, w_bf, tm=tm_s)

        qn_t = jnp.tile(xa_qnorm[l], XA_HEADS)[None, :]
        xo_p = _mem_attend(pp_p, mk_all[l], mv_all[l], qn_t, bd64, groups=bp, t=seq, tt=1024, gs=1, rg=128, col_blk=xq_blk)
        to_hdn = lambda a: jnp.transpose(a, (0, 2, 3, 1)).reshape(ds, XA_WIDTH, N_MEM)
        xo_s = _mem_attend(pp_s, to_hdn(cache_mem_k[l]), to_hdn(cache_mem_v[l]),
                           qn_t, bd64, groups=ds, t=dseq, tt=dseq, gs=4, rg=dseq, col_blk=xq_blk)

        if kind == 0:
            hp = jnp.concatenate([_lane_row(a_log[j], 8), _lane_row(a_dt_bias[j], 8)], axis=0)
            nw = a_norm_w[j][None, :]
            conv0_p = jnp.zeros((bp, 8, cc), F32)
            conv0_s = jnp.pad(state_a_conv[j], ((0, 0), (5, 0), (0, 0)))
            mo_p, st_p = _gdn(pp_p, conv0_p, jnp.zeros((bp, N_HEADS128, HEAD128, HEAD128), F32), a_conv_w[j], hp, nw,
                              groups=bp, t=seq, r=GDN_ROWS, c=GDN_CHUNK)
            mo_s, st_s = _gdn(pp_s, conv0_s, state_a_ssm[j], a_conv_w[j], hp, nw,
                              groups=ds, t=dseq, r=dseq, c=dseq)
            outs.setdefault("a_conv_p", []).append(pp_p.reshape(bp, seq, -1)[:, seq - 3:, :cc])
            outs.setdefault("a_conv_s", []).append(pp_s.reshape(ds, dseq, -1)[:, dseq - 3:, :cc])
            outs.setdefault("a_ssm_p", []).append(st_p)
            outs.setdefault("a_ssm_s", []).append(st_s)
        elif kind == 1:
            lb = lower_bounds[l][None, :]
            nw = b_norm_w[j][None, :]
            mo_p, st_p = _hgrn(pp_p, jnp.zeros((bp, N_HEADS128, HEAD128, HEAD128), F32), lb, nw,
                               groups=bp, t=seq, r=HG_ROWS, c=HG_CHUNK, row_off=0)
            mo_s, st_s = _hgrn(pp_s, state_b_ssm[j], lb, nw, groups=ds, t=dseq, r=dseq, c=dseq, row_off=0)
            outs.setdefault("b_ssm_p", []).append(st_p)
            outs.setdefault("b_ssm_s", []).append(st_s)
        elif kind == 2:
            qn_f = jnp.tile(c_qnorm[j], FOX_HEADS)[None, :]
            kn_f = jnp.tile(c_knorm[j], FOX_HEADS)[None, :]
            fb = _lane_row(c_fbias[j])
            q_p, kt_p, vt_p, lft_p, fc_p, kb_p = _fox_prep_t(pp_p, qn_f, kn_f, fb, bd64, groups=bp, t=seq, tr=256)
            q_s, k_s, lf_s, _ = _fox_prep(pp_s, qn_f, kn_f, fb, bd64, groups=ds, t=dseq, tr=dseq, row_off=0)
            mo_p = _fox_flash(q_p, kt_p, kb_p, pp_p, fc_p, groups=bp, t=seq, tq=FOX_TQ)
            pos_minor = lambda a: jnp.transpose(a, (0, 2, 3, 1))
            mo_s = _fox_sample(page_table, pos_minor(cache_c_k[j]), pos_minor(cache_c_v[j]),
                               jnp.swapaxes(cache_c_logf[j], 1, 2), q_s, k_s, pp_s, lf_s,
                               tnew=dseq, row_off=0, gp=PAGES_PER_STEP)
            time_major = lambda a: jnp.transpose(a.reshape(bp, FOX_HEADS, FOX_DIM, seq), (0, 3, 1, 2))
            outs.setdefault("c_k_p", []).append(time_major(kt_p))
            outs.setdefault("c_v_p", []).append(time_major(vt_p))
            outs.setdefault("c_lf_p", []).append(jnp.swapaxes(lft_p, 1, 2))
            outs.setdefault("c_k_s", []).append(k_s.reshape(ds, dseq, FOX_HEADS, FOX_DIM))
            outs.setdefault("c_v_s", []).append(pp_s[:, 2 * w1024:3 * w1024].reshape(ds, dseq, FOX_HEADS, FOX_DIM))
            outs.setdefault("c_lf_s", []).append(lf_s[:, :FOX_HEADS].reshape(ds, dseq, FOX_HEADS))
        else:
            gd = w1024 // CM_GROUPS
            lg, lbias = d_ln_g[j][None, :], d_ln_b[j][None, :]
            (mo_p,) = _chunk_mlp(pp_p, lg, lbias, d_ws[j], jnp.broadcast_to(d_bs[j][:, :, None], (CM_GROUPS, CM_CHUNK, gd)),
                                 nrows=np_rows, row_off=0, emit_v=False)
            mo_s, v_s = _chunk_mlp(pp_s, lg, lbias, d_ws[j][:, :dseq, :dseq],
                                   jnp.broadcast_to(d_bs[j][:, :dseq, None], (CM_GROUPS, dseq, gd)), nrows=ns_rows, row_off=0,
                                   emit_v=True)
            outs.setdefault("d_v_s", []).append(v_s.reshape(ds, dseq, w1024))

        wo = w_out[l].astype(BF16)
        mlp_w = (wo[:w1024], wo[w1024:], norm_mlp[l][None, :], w_up[l].astype(BF16), w_down[l].astype(BF16))
        h_p = _out_mlp(h_p, mo_p, xo_p, *mlp_w, tm=tm_p)
        h_s = _out_mlp(h_s, mo_s, xo_s, *mlp_w, tm=tm_s)

    st = lambda name: jnp.stack(outs[name])
    mem_out = lambda a: jnp.transpose(a.reshape(depth, bp, XA_HEADS, XA_DIM, N_MEM), (0, 1, 4, 2, 3))
    return (h_p.reshape(bp, seq, d), h_s.reshape(ds, dseq, d),
            st("a_conv_p"), st("a_conv_s"), st("a_ssm_p"), st("a_ssm_s"),
            st("b_ssm_p"), st("b_ssm_s"),
            st("c_k_p"), st("c_v_p"), st("c_lf_p"), st("c_k_s"), st("c_v_s"), st("c_lf_s"),
            st("d_v_s"), mem_out(mk_all), mem_out(mv_all))
```

```python
import functools
import math

import jax
import jax.numpy as jnp
from jax import lax
from jax.experimental import pallas as pl
from jax.experimental.pallas import tpu as pltpu

F32 = jnp.float32
BF16 = jnp.bfloat16
EPS = 1e-6
LOG2E = 1.4426950408889634

LANES = 128
SUBLANES = 8
VMEM_LIMIT = 56 * 1024 * 1024

XA_HEADS, XA_DIM, N_MEM = 4, 64, 256
XA_WIDTH = XA_HEADS * XA_DIM
HEAD128 = 128
N_HEADS128 = 8
FOX_DIM, FOX_HEADS = 64, 16
PAGE = 128
CM_CHUNK, CM_GROUPS = 128, 8
GDN_CHUNK, HG_CHUNK, HG_SUB = 64, 32, 8
GDN_ROWS = 512
HG_ROWS = 256
FOX_TQ = 512
FOX_RG = 128
FOX_BIAS_ROWS = 16
PAGES_PER_STEP = 16


def _cparams(sem):
    return pltpu.CompilerParams(dimension_semantics=sem, vmem_limit_bytes=VMEM_LIMIT)


def _iota(shape, dim):
    return lax.broadcasted_iota(jnp.int32, shape, dim)


def _split3(x):
    hi = x.astype(BF16)
    r = x - hi.astype(F32)
    mid = r.astype(BF16)
    lo = (r - mid.astype(F32)).astype(BF16)
    return hi, mid, lo


def _split2(x):
    hi = x.astype(BF16)
    lo = (x - hi.astype(F32)).astype(BF16)
    return hi, lo


_NN = (((1,), (0,)), ((), ()))
_NT = (((1,), (1,)), ((), ()))
_TN = (((0,), (0,)), ((), ()))


def _mm(a, b, dims=_NN):
    return lax.dot_general(a.astype(BF16), b.astype(BF16), dims, preferred_element_type=F32)


def _sel_l(m, x, dims=_NN):
    out = None
    for p in _split3(x):
        t = lax.dot_general(m, p, dims, preferred_element_type=F32)
        out = t if out is None else out + t
    return out


def _sel_r(x, m, dims=_NN):
    out = None
    for p in _split3(x):
        t = lax.dot_general(p, m, dims, preferred_element_type=F32)
        out = t if out is None else out + t
    return out


def _mm3(a, b):
    ah, al = _split2(a)
    bh, bl = _split2(b)
    d = functools.partial(jnp.dot, preferred_element_type=F32)
    return d(ah, bh) + d(ah, bl) + d(al, bh)


def _sigmoid(x):
    return jax.nn.sigmoid(x)


def _silu(x):
    return x * jax.nn.sigmoid(x)


def _log_sigmoid(x):
    return jnp.minimum(x, 0.0) - jnp.log1p(jnp.exp(-jnp.abs(x)))


def _logaddexp(a, b):
    return jnp.maximum(a, b) + jnp.log1p(jnp.exp(-jnp.abs(a - b)))


def _rms_rows(x, g):
    return x * lax.rsqrt(jnp.mean(x * x, axis=-1, keepdims=True) + EPS) * g


def _group_rms(x, gain, bd, gsz):
    w = x.shape[1]
    x2 = x * x
    parts = []
    for c in range(0, w, 256):
        parts.append(_sel_r(x2[:, c:c + 256], bd))
    ms = (parts[0] if len(parts) == 1 else jnp.concatenate(parts, axis=1)) * (1.0 / gsz)
    return x * lax.rsqrt(ms + EPS) * gain


def _tri(n, kind):
    r, c = _iota((n, n), 0), _iota((n, n), 1)
    if kind == "incl":
        return r >= c
    if kind == "strict":
        return r > c
    if kind == "upper_incl":
        return r <= c
    if kind == "upper_strict":
        return r < c
    raise ValueError(kind)


def _b01(mask):
    return jnp.where(mask, 1.0, 0.0).astype(BF16)


def _proj_body(x_ref, g_ref, w_ref, o_ref, *, tn):
    xn = _rms_rows(x_ref[...], g_ref[...]).astype(BF16)
    wp = w_ref.shape[1]
    for c in range(0, wp, tn):
        e = min(c + tn, wp)
        o_ref[:, c:e] = jnp.dot(xn, w_ref[:, c:e], preferred_element_type=F32)


def _norm_proj(h, g, w, *, tm, tn=512):
    n, d = h.shape
    wp = w.shape[1]
    return pl.pallas_call(
        functools.partial(_proj_body, tn=tn),
        out_shape=jax.ShapeDtypeStruct((n, wp), F32),
        grid=(n // tm,),
        in_specs=[pl.BlockSpec((tm, d), lambda i: (i, 0)),
                  pl.BlockSpec((1, d), lambda i: (0, 0)),
                  pl.BlockSpec((d, wp), lambda i: (0, 0), pipeline_mode=pl.Buffered(1))],
        out_specs=pl.BlockSpec((tm, wp), lambda i: (i, 0)),
        compiler_params=_cparams(("parallel",)),
        name="norm_proj",
    )(h, g, w)


def _out_mlp_body(h_ref, mo_ref, xo_ref, wo1_ref, wo2_ref, g_ref, wu_ref, wd_ref, o_ref, *, tf):
    h2 = (h_ref[...]
          + jnp.dot(mo_ref[...].astype(BF16), wo1_ref[...], preferred_element_type=F32)
          + jnp.dot(xo_ref[...].astype(BF16), wo2_ref[...], preferred_element_type=F32))
    xn = _rms_rows(h2, g_ref[...]).astype(BF16)
    acc = None
    dff = wu_ref.shape[1]
    for c in range(0, dff, tf):
        up = jnp.dot(xn, wu_ref[:, c:c + tf], preferred_element_type=F32)
        act = jnp.square(jnp.maximum(up, 0.0)).astype(BF16)
        t = jnp.dot(act, wd_ref[c:c + tf, :], preferred_element_type=F32)
        acc = t if acc is None else acc + t
    o_ref[...] = h2 + acc


def _out_mlp(h, mo, xo, wo1, wo2, g, wu, wd, *, tm, tf=512):
    n, d = h.shape
    dff = wu.shape[1]
    xw = xo.shape[1]
    const = lambda i: (0, 0)
    one = pl.Buffered(1)
    return pl.pallas_call(
        functools.partial(_out_mlp_body, tf=tf),
        out_shape=jax.ShapeDtypeStruct((n, d), F32),
        grid=(n // tm,),
        in_specs=[pl.BlockSpec((tm, d), lambda i: (i, 0)),
                  pl.BlockSpec((tm, d), lambda i: (i, 0)),
                  pl.BlockSpec((tm, xw), lambda i: (i, 0)),
                  pl.BlockSpec((d, d), const, pipeline_mode=one),
                  pl.BlockSpec((xw, d), const, pipeline_mode=one),
                  pl.BlockSpec((1, d), const),
                  pl.BlockSpec((d, dff), const, pipeline_mode=one),
                  pl.BlockSpec((dff, d), const, pipeline_mode=one)],
        out_specs=pl.BlockSpec((tm, d), lambda i: (i, 0)),
        compiler_params=_cparams(("parallel",)),
        name="out_mlp",
    )(h, mo, xo, wo1, wo2, g, wu, wd)


def _memkv_body(x_ref, g_ref, wt_ref, kn_ref, k_ref, v_ref):
    xn = _rms_rows(x_ref[...], g_ref[...]).astype(BF16)
    kvt = lax.dot_general(wt_ref[...], xn, _NT, preferred_element_type=F32)
    kn = kn_ref[...]
    for hd in range(XA_HEADS):
        kt = kvt[hd * XA_DIM:(hd + 1) * XA_DIM, :]
        ms = jnp.mean(kt * kt, axis=0, keepdims=True)
        k_ref[hd * XA_DIM:(hd + 1) * XA_DIM, :] = kt * lax.rsqrt(ms + EPS) * kn
    v_ref[...] = kvt[XA_WIDTH:, :]


def _memory_kv(mem, mem_norm, w_kv_t, knorm_col):
    b, nm, d = mem.shape
    nl = w_kv_t.shape[0]
    out = jax.ShapeDtypeStruct((nl, b, XA_WIDTH, nm), F32)
    return pl.pallas_call(
        _memkv_body,
        out_shape=(out, out),
        grid=(nl, b),
        in_specs=[pl.BlockSpec((None, nm, d), lambda l, i: (i, 0, 0)),
                  pl.BlockSpec((None, 1, d), lambda l, i: (l, 0, 0)),
                  pl.BlockSpec((None, 2 * XA_WIDTH, d), lambda l, i: (l, 0, 0)),
                  pl.BlockSpec((None, XA_DIM, 1), lambda l, i: (l, 0, 0))],
        out_specs=(pl.BlockSpec((None, None, XA_WIDTH, nm), lambda l, i: (l, i, 0, 0)),
                   pl.BlockSpec((None, None, XA_WIDTH, nm), lambda l, i: (l, i, 0, 0))),
        compiler_params=_cparams(("parallel", "parallel")),
        name="memory_kv",
    )(mem, mem_norm, w_kv_t, knorm_col)


def _xattn_body(q_ref, mk_ref, mv_ref, qn_ref, bd_ref, o_ref, *, gs, tt, rg):
    q = _group_rms(q_ref[...], qn_ref[...], bd_ref[...], XA_DIM) * (XA_DIM ** -0.5)
    lane = _iota((1, XA_WIDTH), 1) // XA_DIM
    row = _iota((XA_WIDTH, 1), 0) // XA_DIM
    mkt = [mk_ref[g].astype(BF16) for g in range(gs)]
    mvt = [[jnp.where(row == hd, mv_ref[g], 0.0).astype(BF16) for hd in range(XA_HEADS)] for g in range(gs)]
    chains = [(g, r0, hd) for g in range(gs) for r0 in range(0, tt, rg) for hd in range(XA_HEADS)]
    qc = [jnp.where(lane == hd, q[g * tt + r0:g * tt + r0 + rg, :], 0.0).astype(BF16) for (g, r0, hd) in chains]
    s = [jnp.dot(q_, mkt[g], preferred_element_type=F32) for q_, (g, _, _) in zip(qc, chains)]
    p = [jnp.exp(s_ - jnp.max(s_, axis=-1, keepdims=True)) for s_ in s]
    inv = [1.0 / jnp.sum(p_, axis=-1, keepdims=True) for p_ in p]
    t = [lax.dot_general(p_.astype(BF16), mvt[g][hd], _NT, preferred_element_type=F32) * i_
         for p_, i_, (g, _, hd) in zip(p, inv, chains)]
    for c0 in range(0, len(chains), XA_HEADS):
        g, r0, _ = chains[c0]
        o_ref[g * tt + r0:g * tt + r0 + rg, :] = (t[c0] + t[c0 + 1]) + (t[c0 + 2] + t[c0 + 3])


def _mem_attend(pp, mk, mv, qn_t, bd64, *, groups, t, tt, gs, rg, col_blk):
    nt = t // tt
    assert gs == 1 or nt == 1
    return pl.pallas_call(
        functools.partial(_xattn_body, gs=gs, tt=tt, rg=rg),
        out_shape=jax.ShapeDtypeStruct((groups * t, XA_WIDTH), F32),
        grid=(groups // gs, nt),
        in_specs=[pl.BlockSpec((gs * tt, XA_WIDTH), lambda g, i: (g * nt + i, col_blk)),
                  pl.BlockSpec((gs, XA_WIDTH, N_MEM), lambda g, i: (g, 0, 0)),
                  pl.BlockSpec((gs, XA_WIDTH, N_MEM), lambda g, i: (g, 0, 0)),
                  pl.BlockSpec((1, XA_WIDTH), lambda g, i: (0, 0)),
                  pl.BlockSpec((256, 256), lambda g, i: (0, 0))],
        out_specs=pl.BlockSpec((gs * tt, XA_WIDTH), lambda g, i: (g * nt + i, 0)),
        compiler_params=_cparams(("parallel", "parallel")),
        name="mem_attend",
    )(pp, mk, mv, qn_t, bd64)


def _gdn_body(qkv_ref, z_ref, ba_ref, conv0_ref, s0_ref, cw_ref, hp_ref, nw_ref,
              o_ref, sout_ref, xbuf, cbuf, s_scr, *, r, c):
    j = pl.program_id(1)
    kw = N_HEADS128 * HEAD128

    @pl.when(j == 0)
    def _():
        xbuf[0:8, :] = conv0_ref[...]
        s_scr[...] = s0_ref[...]

    x = qkv_ref[...]
    xbuf[8:8 + r, :] = x
    cw = cw_ref[...]
    conv = x * cw[3:4, :]
    for i in range(1, 4):
        conv = conv + xbuf[8 - i:8 - i + r, :] * cw[3 - i:4 - i, :]
    cbuf[...] = _silu(conv)
    xbuf[0:8, :] = xbuf[r:r + 8, :]

    hp = hp_ref[...]
    incl = _tri(c, "incl")
    strict = _tri(c, "strict")
    cum_l = _b01(incl)
    cum_u = _b01(_tri(c, "upper_incl"))
    eye = jnp.where(_iota((c, c), 0) == _iota((c, c), 1), 1.0, 0.0)
    nw = nw_ref[...]
    heads = range(N_HEADS128)
    lo = [hd * HEAD128 for hd in heads]

    pre = []
    for ch in range(r // c):
        rows = slice(ch * c, (ch + 1) * c)
        ba = ba_ref[rows, :]
        beta_all = _sigmoid(ba)
        g_all = -jnp.exp(hp[0:1, :]) * jax.nn.softplus(ba + hp[1:2, :])
        gcum = _sel_l(cum_l, g_all)
        gcum_t = _sel_r(g_all, cum_u, _TN)
        eg_all = jnp.exp(gcum)
        q = [cbuf[rows, l:l + HEAD128] for l in lo]
        k = [cbuf[rows, kw + l:kw + l + HEAD128] for l in lo]
        v = [cbuf[rows, 2 * kw + l:2 * kw + l + HEAD128] for l in lo]
        q = [x_ * lax.rsqrt(jnp.sum(x_ * x_, axis=-1, keepdims=True) + EPS) * (HEAD128 ** -0.5) for x_ in q]
        k = [x_ * lax.rsqrt(jnp.sum(x_ * x_, axis=-1, keepdims=True) + EPS) for x_ in k]
        bcol = [beta_all[:, hd:hd + 1] for hd in heads]
        gc = [gcum[:, 8 + hd:9 + hd] for hd in heads]
        egc = [eg_all[:, 8 + hd:9 + hd] for hd in heads]
        gl = [gcum[c - 1:c, 8 + hd:9 + hd] for hd in heads]
        decay = [jnp.exp(jnp.where(incl, gc[hd] - gcum_t[8 + hd:9 + hd, :], -jnp.inf)) for hd in heads]
        kk = [_mm(k[hd], k[hd], _NT) for hd in heads]
        qk = [_mm(q[hd], k[hd], _NT) for hd in heads]
        a = [jnp.where(strict, bcol[hd] * kk[hd] * decay[hd], 0.0) for hd in heads]
        inv = [eye - a_ for a_ in a]
        pw = [_mm3(a_, a_) for a_ in a]
        inv = [inv[hd] + _mm3(inv[hd], pw[hd]) for hd in heads]
        n = 2
        while 2 * n < c:
            pw = [_mm3(p_, p_) for p_ in pw]
            inv = [inv[hd] + _mm3(inv[hd], pw[hd]) for hd in heads]
            n *= 2
        pre.append(dict(
            sol_v=[_mm3(inv[hd], v[hd] * bcol[hd]) for hd in heads],
            sol_k=[_mm3(inv[hd], k[hd] * (bcol[hd] * egc[hd])) for hd in heads],
            att=[qk[hd] * decay[hd] for hd in heads],
            qe=[q[hd] * egc[hd] for hd in heads],
            ke=[k[hd] * jnp.exp(gl[hd] - gc[hd]) for hd in heads],
            egl=[jnp.exp(gl[hd]) for hd in heads]))

    s = [s_scr[hd] for hd in heads]
    for ch, p_ in enumerate(pre):
        rows = slice(ch * c, (ch + 1) * c)
        u = [p_["sol_v"][hd] - _mm(p_["sol_k"][hd], s[hd]) for hd in heads]
        o = [_mm(p_["qe"][hd], s[hd]) + _mm(p_["att"][hd], u[hd]) for hd in heads]
        s = [s[hd] * p_["egl"][hd] + _mm(p_["ke"][hd], u[hd], _TN) for hd in heads]
        for hd in heads:
            o_ref[rows, lo[hd]:lo[hd] + HEAD128] = _rms_rows(o[hd], nw) * _silu(z_ref[rows, lo[hd]:lo[hd] + HEAD128])
    for hd in heads:
        s_scr[hd] = s[hd]

    @pl.when(j == pl.num_programs(1) - 1)
    def _():
        sout_ref[...] = s_scr[...]


def _gdn(pp, conv0, s0, cw, hp, nw, *, groups, t, r, c):
    nt = t // r
    cc = 3 * N_HEADS128 * HEAD128
    vw = N_HEADS128 * HEAD128
    return pl.pallas_call(
        functools.partial(_gdn_body, r=r, c=c),
        out_shape=(jax.ShapeDtypeStruct((groups * t, vw), F32),
                   jax.ShapeDtypeStruct((groups, N_HEADS128, HEAD128, HEAD128), F32)),
        grid=(groups, nt),
        in_specs=[pl.BlockSpec((r, cc), lambda g, i: (g * nt + i, 0)),
                  pl.BlockSpec((r, vw), lambda g, i: (g * nt + i, cc // vw)),
                  pl.BlockSpec((r, LANES), lambda g, i: (g * nt + i, (cc + vw + XA_WIDTH) // LANES)),
                  pl.BlockSpec((None, 8, cc), lambda g, i: (g, 0, 0)),
                  pl.BlockSpec((None, N_HEADS128, HEAD128, HEAD128), lambda g, i: (g, 0, 0, 0)),
                  pl.BlockSpec((4, cc), lambda g, i: (0, 0)),
                  pl.BlockSpec((2, LANES), lambda g, i: (0, 0)),
                  pl.BlockSpec((1, HEAD128), lambda g, i: (0, 0))],
        out_specs=(pl.BlockSpec((r, vw), lambda g, i: (g * nt + i, 0)),
                   pl.BlockSpec((None, N_HEADS128, HEAD128, HEAD128), lambda g, i: (g, 0, 0, 0))),
        scratch_shapes=[pltpu.VMEM((r + 8, cc), F32), pltpu.VMEM((r, cc), F32),
                        pltpu.VMEM((N_HEADS128, HEAD128, HEAD128), F32)],
        compiler_params=_cparams(("parallel", "arbitrary")),
        name="gdn",
    )(pp, pp, pp, conv0, s0, cw, hp, nw)


def _hgrn_body(q_ref, f_ref, i_ref, g_ref, s0_ref, lb_ref, nw_ref, o_ref, sout_ref, st_scr, *, r, c, gs):
    j = pl.program_id(1)

    @pl.when(j == 0)
    def _():
        for g in range(gs):
            for hd in range(N_HEADS128):
                st_scr[g * N_HEADS128 + hd] = s0_ref[g, hd].T

    nw = nw_ref[...]
    cum_m = _b01(_tri(c, "incl"))
    nsub = c // HG_SUB
    tri8 = _iota((HG_SUB, 1), 0)
    heads = range(N_HEADS128)
    lo = [hd * HEAD128 for hd in heads]
    lb = [lb_ref[:, l:l + HEAD128] for l in lo]
    log_lb = [jnp.log(x_) for x_ in lb]
    log_1m = [jnp.log1p(-x_) for x_ in lb]
    qe_all, upd_all, ebl_all, intra_all = [], [], [], []
    for ch in range(r // c):
        r0 = ch * c
        ff = [f_ref[r0:r0 + c, l:l + HEAD128] for l in lo]
        logf = [_logaddexp(log_lb[hd], log_1m[hd] + _log_sigmoid(ff[hd])) for hd in heads]
        k = [(1.0 - lb[hd]) * _sigmoid(-ff[hd]) for hd in heads]
        q = [_silu(q_ref[r0:r0 + c, l:l + HEAD128]) * (HEAD128 ** -0.5) for l in lo]
        v = [i_ref[r0:r0 + c, l:l + HEAD128] for l in lo]
        bc = [_sel_l(cum_m, x_) for x_ in logf]
        bl = [bc[hd][c - 1:c, :] for hd in heads]
        qe_all.append([q[hd] * jnp.exp(bc[hd]) for hd in heads])
        ebl_all.append([jnp.exp(bl[hd]) for hd in heads])
        upd_all.append([_mm(v[hd], k[hd] * jnp.exp(bl[hd] - bc[hd]), _TN) for hd in heads])
        blocks = [[None] for hd in heads]
        for sb in range(1, nsub):
            b0 = sb * HG_SUB
            for hd in heads:
                bref = bc[hd][b0 - 1:b0, :]
                qs = q[hd][b0:b0 + HG_SUB, :] * jnp.exp(bc[hd][b0:b0 + HG_SUB, :] - bref)
                ks = k[hd][0:b0, :] * jnp.exp(bref - bc[hd][0:b0, :])
                att = _mm(qs, ks, _NT)
                blocks[hd].append(_mm(att, v[hd][0:b0, :]))
        bc2 = [bc[hd] * LOG2E for hd in heads]
        for sb in range(nsub):
            b0 = sb * HG_SUB
            for hd in heads:
                qb, bcb = q[hd][b0:b0 + HG_SUB, :], bc2[hd][b0:b0 + HG_SUB, :]
                ob = blocks[hd][sb]
                for s_ in range(HG_SUB):
                    w = jnp.exp2(bcb - bc2[hd][b0 + s_:b0 + s_ + 1, :])
                    col = jnp.sum(qb * k[hd][b0 + s_:b0 + s_ + 1, :] * w, axis=-1, keepdims=True)
                    if s_ > 0:
                        col = jnp.where(tri8 >= s_, col, 0.0)
                    t_ = col * v[hd][b0 + s_:b0 + s_ + 1, :]
                    ob = t_ if ob is None else ob + t_
                blocks[hd][sb] = ob
        intra_all.append([blocks[hd][0] if nsub == 1 else jnp.concatenate(blocks[hd], axis=0) for hd in heads])
    st = [st_scr[hd] for hd in heads]
    for ch in range(r // c):
        r0 = ch * c
        if gs > 1:
            st = [st_scr[ch * N_HEADS128 + hd] for hd in heads]
        o = [intra_all[ch][hd] + _mm(qe_all[ch][hd], st[hd], _NT) for hd in heads]
        st = [st[hd] * ebl_all[ch][hd] + upd_all[ch][hd] for hd in heads]
        for hd in heads:
            o_ref[r0:r0 + c, lo[hd]:lo[hd] + HEAD128] = _rms_rows(o[hd], nw) * _silu(g_ref[r0:r0 + c, lo[hd]:lo[hd] + HEAD128])
        if gs > 1:
            for hd in heads:
                st_scr[ch * N_HEADS128 + hd] = st[hd]
    if gs == 1:
        for hd in heads:
            st_scr[hd] = st[hd]

    @pl.when(j == pl.num_programs(1) - 1)
    def _():
        for g in range(gs):
            for hd in range(N_HEADS128):
                sout_ref[g, hd] = st_scr[g * N_HEADS128 + hd].T


def _hgrn(pp, s0, lb, nw, *, groups, t, r, c, row_off, gs=1):
    assert gs == 1 or (t == c and r == gs * c and row_off == 0)
    nt = 1 if gs > 1 else t // r
    off = row_off // r
    w = N_HEADS128 * HEAD128
    rows = lambda col: pl.BlockSpec((r, w), lambda g, i: (off + g * nt + i, col))
    state = pl.BlockSpec((gs, N_HEADS128, HEAD128, HEAD128), lambda g, i: (g, 0, 0, 0))
    return pl.pallas_call(
        functools.partial(_hgrn_body, r=r, c=c, gs=gs),
        out_shape=(jax.ShapeDtypeStruct((groups * t, w), F32),
                   jax.ShapeDtypeStruct((groups, N_HEADS128, HEAD128, HEAD128), F32)),
        grid=(groups // gs, nt),
        in_specs=[rows(0), rows(1), rows(2), rows(3), state,
                  pl.BlockSpec((1, w), lambda g, i: (0, 0)),
                  pl.BlockSpec((1, HEAD128), lambda g, i: (0, 0))],
        out_specs=(pl.BlockSpec((r, w), lambda g, i: (g * nt + i, 0)), state),
        scratch_shapes=[pltpu.VMEM((gs * N_HEADS128, HEAD128, HEAD128), F32)],
        compiler_params=_cparams(("parallel", "arbitrary")),
        name="hgrn2",
    )(pp, pp, pp, pp, s0, lb, nw)


def _fox_prep_body(q_ref, k_ref, fl_ref, qn_ref, kn_ref, fb_ref, bd_ref, qo_ref, ko_ref, lf_ref, fc_ref, carry):
    j = pl.program_id(1)

    @pl.when(j == 0)
    def _():
        carry[...] = jnp.zeros_like(carry)

    bd = bd_ref[...]
    qo_ref[...] = _group_rms(q_ref[...], qn_ref[...], bd, FOX_DIM)
    ko_ref[...] = _group_rms(k_ref[...], kn_ref[...], bd, FOX_DIM)
    lf = jax.nn.log_sigmoid(fl_ref[...] + fb_ref[...])
    lf_ref[...] = lf
    tr = lf.shape[0]
    fc = _sel_l(_b01(_tri(tr, "incl")), lf) + carry[...]
    fc_ref[...] = fc
    carry[...] = fc[tr - 1:tr, :]


def _fox_prep(pp, qn_t, kn_t, fb, bd64, *, groups, t, tr, row_off):
    nt = t // tr
    off = row_off // tr
    w = FOX_HEADS * FOX_DIM
    n = groups * t
    big = jax.ShapeDtypeStruct((n, w), F32)
    small = jax.ShapeDtypeStruct((n, LANES), F32)
    return pl.pallas_call(
        _fox_prep_body,
        out_shape=(big, big, small, small),
        grid=(groups, nt),
        in_specs=[pl.BlockSpec((tr, w), lambda g, i: (off + g * nt + i, 0)),
                  pl.BlockSpec((tr, w), lambda g, i: (off + g * nt + i, 1)),
                  pl.BlockSpec((tr, LANES), lambda g, i: (off + g * nt + i, (4 * w + XA_WIDTH) // LANES)),
                  pl.BlockSpec((1, w), lambda g, i: (0, 0)),
                  pl.BlockSpec((1, w), lambda g, i: (0, 0)),
                  pl.BlockSpec((1, LANES), lambda g, i: (0, 0)),
                  pl.BlockSpec((256, 256), lambda g, i: (0, 0))],
        out_specs=(pl.BlockSpec((tr, w), lambda g, i: (g * nt + i, 0)),
                   pl.BlockSpec((tr, w), lambda g, i: (g * nt + i, 0)),
                   pl.BlockSpec((tr, LANES), lambda g, i: (g * nt + i, 0)),
                   pl.BlockSpec((tr, LANES), lambda g, i: (g * nt + i, 0))),
        scratch_shapes=[pltpu.VMEM((1, LANES), F32)],
        compiler_params=_cparams(("parallel", "arbitrary")),
        name="fox_prep",
    )(pp, pp, pp, qn_t, kn_t, fb, bd64)


def _fox_prep_t_body(q_ref, k_ref, v_ref, fl_ref, qn_ref, kn_ref, fb_ref, bd_ref,
                     qo_ref, kt_ref, vt_ref, lft_ref, fc_ref, kb_ref, carry):
    j = pl.program_id(1)

    @pl.when(j == 0)
    def _():
        carry[...] = jnp.zeros_like(carry)

    bd = bd_ref[...]
    qo_ref[...] = _group_rms(q_ref[...], qn_ref[...], bd, FOX_DIM)
    kt_ref[...] = _group_rms(k_ref[...], kn_ref[...], bd, FOX_DIM).T
    vt_ref[...] = v_ref[...].T
    lf = jax.nn.log_sigmoid(fl_ref[...] + fb_ref[...])
    tr = lf.shape[0]
    fc = _sel_l(_b01(_tri(tr, "incl")), lf) + carry[...]
    fc_ref[...] = fc
    carry[...] = fc[tr - 1:tr, :]
    lft_ref[...] = lf.T[0:FOX_HEADS, :]
    rr = _iota((LANES, LANES), 0) % FOX_BIAS_ROWS
    first = 2 * (_iota((LANES, LANES), 0) // FOX_BIAS_ROWS)
    lanei = _iota((LANES, LANES), 1)
    kbias = jnp.where((_iota((LANES, 1), 0) % FOX_BIAS_ROWS) // 3 == 1, 1.0, 0.0)
    for pi, part in enumerate(_split3(fc * LOG2E)):
        pick = ((rr == pi) & (lanei == first)) | ((rr == 6 + pi) & (lanei == first + 1))
        kbias = kbias + lax.dot_general(jnp.where(pick, -1.0, 0.0).astype(BF16), part, _NT, preferred_element_type=F32)
    kb_ref[...] = kbias.astype(BF16)


def _fox_prep_t(pp, qn_t, kn_t, fb, bd64, *, groups, t, tr):
    nt = t // tr
    w = FOX_HEADS * FOX_DIM
    n = groups * t
    rows = lambda col: pl.BlockSpec((tr, w), lambda g, i: (g * nt + i, col))
    const = lambda g, i: (0, 0)
    tmaj = jax.ShapeDtypeStruct((groups, w, t), F32)
    hmaj = jax.ShapeDtypeStruct((groups, FOX_HEADS, t), F32)
    return pl.pallas_call(
        _fox_prep_t_body,
        out_shape=(jax.ShapeDtypeStruct((n, w), F32), tmaj, tmaj, hmaj, jax.ShapeDtypeStruct((n, LANES), F32),
                   jax.ShapeDtypeStruct((groups, LANES, t), BF16)),
        grid=(groups, nt),
        in_specs=[rows(0), rows(1), rows(2),
                  pl.BlockSpec((tr, LANES), lambda g, i: (g * nt + i, (4 * w + XA_WIDTH) // LANES)),
                  pl.BlockSpec((1, w), const), pl.BlockSpec((1, w), const), pl.BlockSpec((1, LANES), const),
                  pl.BlockSpec((256, 256), const)],
        out_specs=(pl.BlockSpec((tr, w), lambda g, i: (g * nt + i, 0)),
                   pl.BlockSpec((None, w, tr), lambda g, i: (g, 0, i)),
                   pl.BlockSpec((None, w, tr), lambda g, i: (g, 0, i)),
                   pl.BlockSpec((None, FOX_HEADS, tr), lambda g, i: (g, 0, i)),
                   pl.BlockSpec((tr, LANES), lambda g, i: (g * nt + i, 0)),
                   pl.BlockSpec((None, LANES, tr), lambda g, i: (g, 0, i))),
        scratch_shapes=[pltpu.VMEM((1, LANES), F32)],
        compiler_params=_cparams(("parallel", "arbitrary")),
        name="fox_prep_t",
    )(pp, pp, pp, pp, qn_t, kn_t, fb, bd64)


def _fox_flash_body(q_ref, k_ref, kb_ref, v_ref, og_ref, fq_ref, o_ref, m_scr, l_scr, acc_scr, *, tq):
    hp = pl.program_id(1)
    qi = pl.program_id(2)
    lane = _iota((1, LANES), 1)
    q = q_ref[...] * (FOX_DIM ** -0.5 * LOG2E)
    fq_parts = _split3(fq_ref[...] * LOG2E)
    brow = _iota((LANES, FOX_BIAS_ROWS), 0)
    bcol = _iota((LANES, FOX_BIAS_ROWS), 1)
    ones_at = _iota((1, FOX_BIAS_ROWS), 1) // 3
    qs = []
    for j in range(2):
        qb = jnp.where(ones_at == 2 * j, 1.0, 0.0)
        for pi, part in enumerate(fq_parts):
            pick = _b01((brow == 2 * hp + j) & (bcol == 3 + pi))
            qb = qb + jnp.dot(part, pick, preferred_element_type=F32)
        qm = jnp.where((lane // FOX_DIM) == j, q, 0.0).astype(BF16)
        qs.append(jnp.concatenate([qm, qb.astype(BF16)], axis=1))
        m_scr[j] = jnp.full((tq, LANES), -jnp.inf, F32)
        l_scr[j] = jnp.zeros((tq, LANES), F32)
        acc_scr[j] = jnp.zeros((tq, LANES), F32)

    def rep(x, n):
        return x if n == LANES else jnp.concatenate([x] * (n // LANES), axis=1)

    def block(ki, masked):
        k0 = pl.multiple_of(ki * tq, tq)
        kb = jnp.concatenate([k_ref[:, pl.ds(k0, tq)].astype(BF16),
                              kb_ref[:, pl.ds(k0, tq)]], axis=0)
        vb = v_ref[pl.ds(k0, tq), :].astype(BF16)
        chains = [(j, r0) for r0 in range(0, tq, FOX_RG) for j in range(2)]
        nk = [r0 + FOX_RG if masked else tq for (_, r0) in chains]
        s = [jnp.dot(qs[j][r0:r0 + FOX_RG, :], kb[:, :n_], preferred_element_type=F32)
             for (j, r0), n_ in zip(chains, nk)]
        if masked:
            s = [jnp.where(_iota((FOX_RG, n_), 1) <= _iota((FOX_RG, n_), 0) + r0, s_, -jnp.inf)
                 for (_, r0), n_, s_ in zip(chains, nk, s)]
        m_old = [m_scr[j, r0:r0 + FOX_RG, :] for (j, r0) in chains]
        m_new = [jnp.maximum(mo_, jnp.max(s_, axis=-1, keepdims=True)) for mo_, s_ in zip(m_old, s)]
        alpha = [jnp.exp2(mo_ - mn_) for mo_, mn_ in zip(m_old, m_new)]
        p = [jnp.exp2(s_ - rep(mn_, n_)) for s_, mn_, n_ in zip(s, m_new, nk)]
        pv = [jnp.dot(p_.astype(BF16), vb[:n_, :], preferred_element_type=F32) for p_, n_ in zip(p, nk)]
        for c_, (j, r0) in enumerate(chains):
            rows = slice(r0, r0 + FOX_RG)
            l_scr[j, rows, :] = alpha[c_] * l_scr[j, rows, :] + jnp.sum(p[c_], axis=-1, keepdims=True)
            acc_scr[j, rows, :] = alpha[c_] * acc_scr[j, rows, :] + pv[c_]
            m_scr[j, rows, :] = m_new[c_]

    def body(ki, carry):
        block(ki, False)
        return carry

    lax.fori_loop(0, qi, body, 0)
    block(qi, True)
    o0 = acc_scr[0] / l_scr[0]
    o1 = acc_scr[1] / l_scr[1]
    o_ref[...] = jnp.where(lane < FOX_DIM, o0, o1) * _sigmoid(og_ref[...])


def _fox_flash(qn, kt, kbias, pp, fc, *, groups, t, tq):
    nq = t // tq
    w = FOX_HEADS * FOX_DIM
    hp_n = w // LANES
    vblk = 2 * w // LANES
    gblk = 3 * w // LANES
    return pl.pallas_call(
        functools.partial(_fox_flash_body, tq=tq),
        out_shape=jax.ShapeDtypeStruct((groups * t, w), F32),
        grid=(groups, hp_n, nq),
        in_specs=[pl.BlockSpec((tq, LANES), lambda g, h, i: (g * nq + i, h)),
                  pl.BlockSpec((None, LANES, t), lambda g, h, i: (g, h, 0)),
                  pl.BlockSpec((None, FOX_BIAS_ROWS, t), lambda g, h, i: (g, h, 0)),
                  pl.BlockSpec((t, LANES), lambda g, h, i: (g, vblk + h)),
                  pl.BlockSpec((tq, LANES), lambda g, h, i: (g * nq + i, gblk + h)),
                  pl.BlockSpec((tq, LANES), lambda g, h, i: (g * nq + i, 0))],
        out_specs=pl.BlockSpec((tq, LANES), lambda g, h, i: (g * nq + i, h)),
        scratch_shapes=[pltpu.VMEM((2, tq, LANES), F32)] * 3,
        compiler_params=_cparams(("parallel", "parallel", "arbitrary")),
        name="fox_flash",
    )(qn, kt, kbias, pp, pp, fc)


def _fox_sample_body(pt_ref, *refs, gp, tnew):
    del pt_ref
    k_refs = refs[0:gp]
    v_refs = refs[gp:2 * gp]
    lf_refs = refs[2 * gp:3 * gp]
    q_ref, kn_ref, vn_ref, og_ref, lfn_ref, o_ref, q_scr, m_scr, l_scr, acc, carry = refs[3 * gp:]
    pg = pl.program_id(1)

    @pl.when(pg == 0)
    def _():
        qs = q_ref[...] * (FOX_DIM ** -0.5)
        for hd in range(FOX_HEADS):
            q_scr[hd] = qs[:, hd * FOX_DIM:(hd + 1) * FOX_DIM]
        m_scr[...] = jnp.full(m_scr.shape, -jnp.inf, F32)
        l_scr[...] = jnp.zeros_like(l_scr)
        acc[...] = jnp.zeros_like(acc)
        carry[...] = jnp.zeros_like(carry)

    fnew = _sel_l(_b01(_tri(tnew, "incl")), lfn_ref[...])

    heads = range(FOX_HEADS)

    def update(s, pv_fn):
        m_old = [m_scr[hd] for hd in heads]
        m_new = [jnp.maximum(m_old[hd], jnp.max(s[hd], axis=-1, keepdims=True)) for hd in heads]
        alpha = [jnp.exp(m_old[hd] - m_new[hd]) for hd in heads]
        p = [jnp.exp(s[hd] - m_new[hd]) for hd in heads]
        pv = [pv_fn(hd, p[hd].astype(BF16)) for hd in heads]
        for hd in heads:
            l_scr[hd] = alpha[hd] * l_scr[hd] + jnp.sum(p[hd], axis=-1, keepdims=True)
            acc[hd] = alpha[hd] * acc[hd] + pv[hd]
            m_scr[hd] = m_new[hd]

    later = _b01(_tri(PAGE, "strict"))
    rests = []
    run = carry[...]
    for j in range(gp):
        lft = lf_refs[j][...]
        rests.append(_sel_r(lft, later) + run)
        run = run + jnp.sum(lft, axis=-1, keepdims=True)
    carry[...] = run

    qh = [q_scr[hd].astype(BF16) for hd in heads]
    s = [jnp.concatenate(
        [jnp.dot(qh[hd], k_refs[j][hd].astype(BF16), preferred_element_type=F32) + rests[j][hd:hd + 1, :]
         for j in range(gp)], axis=1) + fnew[:, hd:hd + 1] for hd in heads]

    def pv_past(hd, p):
        parts = [lax.dot_general(p[:, j * PAGE:(j + 1) * PAGE], v_refs[j][hd].astype(BF16), _NT,
                                 preferred_element_type=F32) for j in range(gp)]
        while len(parts) > 1:
            parts = [parts[i] + parts[i + 1] for i in range(0, len(parts), 2)]
        return parts[0]

    update(s, pv_past)

    @pl.when(pg == pl.num_programs(1) - 1)
    def _():
        fnew_t = fnew.T
        causal = _tri(tnew, "incl")
        lo = [hd * FOX_DIM for hd in heads]
        qh = [q_scr[hd].astype(BF16) for hd in heads]
        s = [lax.dot_general(qh[hd], kn_ref[:, lo[hd]:lo[hd] + FOX_DIM].astype(BF16), _NT, preferred_element_type=F32)
             + fnew[:, hd:hd + 1] - fnew_t[hd:hd + 1, :] for hd in heads]
        s = [jnp.where(causal, s_, -jnp.inf) for s_ in s]
        update(s, lambda hd, p: jnp.dot(p, vn_ref[:, lo[hd]:lo[hd] + FOX_DIM].astype(BF16), preferred_element_type=F32))
        o_ref[...] = jnp.concatenate([acc[hd] / l_scr[hd] for hd in heads], axis=1) * _sigmoid(og_ref[...])


def _fox_sample(page_table, ckt, cvt, clft, qn, kn, pp, lfn, *, tnew, row_off, gp):
    ns, npages = page_table.shape
    w = FOX_HEADS * FOX_DIM
    ngrp = npages // gp
    off = row_off // tnew

    def page_map(j, nd):
        return lambda s, g, pt: (pt[s, npages - 1 - (g * gp + j)],) + (0,) * nd

    kv_specs = [pl.BlockSpec((None, FOX_HEADS, FOX_DIM, PAGE), page_map(j, 3)) for j in range(gp)]
    lf_specs = [pl.BlockSpec((None, FOX_HEADS, PAGE), page_map(j, 2)) for j in range(gp)]
    row = lambda s, g, pt: (s, 0)
    grid_spec = pltpu.PrefetchScalarGridSpec(
        num_scalar_prefetch=1,
        grid=(ns, ngrp),
        in_specs=kv_specs + kv_specs + lf_specs + [
            pl.BlockSpec((tnew, w), row),
            pl.BlockSpec((tnew, w), row),
            pl.BlockSpec((tnew, w), lambda s, g, pt: (off + s, 2)),
            pl.BlockSpec((tnew, w), lambda s, g, pt: (off + s, 3)),
            pl.BlockSpec((tnew, LANES), row)],
        out_specs=pl.BlockSpec((tnew, w), row),
        scratch_shapes=[pltpu.VMEM((FOX_HEADS, tnew, FOX_DIM), F32), pltpu.VMEM((FOX_HEADS, tnew, 1), F32),
                        pltpu.VMEM((FOX_HEADS, tnew, 1), F32), pltpu.VMEM((FOX_HEADS, tnew, FOX_DIM), F32),
                        pltpu.VMEM((FOX_HEADS, 1), F32)],
    )
    return pl.pallas_call(
        functools.partial(_fox_sample_body, gp=gp, tnew=tnew),
        out_shape=jax.ShapeDtypeStruct((ns * tnew, w), F32),
        grid_spec=grid_spec,
        compiler_params=_cparams(("parallel", "arbitrary")),
        name="fox_sample",
    )(page_table, *([ckt] * gp), *([cvt] * gp), *([clft] * gp), qn, kn, pp, pp, lfn)


def _cmlp_body(u_ref, v_ref, lg_ref, lbias_ref, ws_ref, bsb_ref, o_ref, *maybe_vo_ref):
    u = jax.nn.gelu(u_ref[...], approximate=True)
    z = jax.nn.gelu(v_ref[...], approximate=True)
    mu = jnp.mean(z, axis=-1, keepdims=True)
    zc = z - mu
    v = zc * lax.rsqrt(jnp.mean(zc * zc, axis=-1, keepdims=True) + EPS) * lg_ref[...] + lbias_ref[...]
    for vo_ref in maybe_vo_ref:
        vo_ref[...] = v
    tril = _tri(ws_ref.shape[1], "incl")
    gd = v.shape[1] // CM_GROUPS
    for g in range(CM_GROUPS):
        wm = jnp.where(tril, ws_ref[g], 0.0)
        mixed = _mm(wm, v[:, g * gd:(g + 1) * gd]) + bsb_ref[g]
        o_ref[:, g * gd:(g + 1) * gd] = u[:, g * gd:(g + 1) * gd] * mixed


def _chunk_mlp(pp, ln_g, ln_b, ws, bsb, *, nrows, row_off, emit_v):
    w = ln_g.shape[1]
    cr = ws.shape[1]
    off = row_off // cr
    out = jax.ShapeDtypeStruct((nrows, w), F32)
    const2 = lambda i: (0, 0)
    const3 = lambda i: (0, 0, 0)
    return pl.pallas_call(
        _cmlp_body,
        out_shape=(out, out) if emit_v else (out,),
        grid=(nrows // cr,),
        in_specs=[pl.BlockSpec((cr, w), lambda i: (off + i, 0)),
                  pl.BlockSpec((cr, w), lambda i: (off + i, 1)),
                  pl.BlockSpec((1, w), const2), pl.BlockSpec((1, w), const2),
                  pl.BlockSpec((CM_GROUPS, cr, cr), const3),
                  pl.BlockSpec((CM_GROUPS, cr, w // CM_GROUPS), const3)],
        out_specs=(pl.BlockSpec((cr, w), lambda i: (i, 0)),) * (2 if emit_v else 1),
        compiler_params=_cparams(("parallel",)),
        name="chunk_mlp",
    )(pp, pp, ln_g, ln_b, ws, bsb)


def _pad_cols(w, mult):
    pad = (-w.shape[1]) % mult
    return jnp.pad(w, ((0, 0), (0, pad))) if pad else w


def _lane_row(v, start=0):
    return jnp.zeros((1, LANES), F32).at[0, start:start + v.shape[0]].set(v.astype(F32))


def kernel(x_prompt, x_sample, mem_prompt, state_a_conv, state_a_ssm, state_b_ssm, cache_c_k, cache_c_v, cache_c_logf, cache_mem_k, cache_mem_v, page_table, norm_mix, w_out, norm_mlp, w_up, w_down, mem_norm, w_mem_kv, xa_qnorm, xa_knorm, w_in_a, a_conv_w, a_log, a_dt_bias, a_norm_w, w_in_b, hg_lb, b_norm_w, w_in_c, c_fbias, c_qnorm, c_knorm, w_in_d, d_ln_g, d_ln_b, d_ws, d_bs):
    bp, seq, d = x_prompt.shape
    ds, dseq, _ = x_sample.shape
    depth = norm_mix.shape[0]
    np_rows, ns_rows = bp * seq, ds * dseq
    tm_p = 512 if np_rows % 512 == 0 else 128
    tm_s = 256 if ns_rows % 256 == 0 else 8
    w1024 = N_HEADS128 * HEAD128
    cc = 3 * w1024

    h_p, h_s = x_prompt.reshape(np_rows, d), x_sample.reshape(ns_rows, d)
    bd64 = (jnp.arange(256)[:, None] // 64 == jnp.arange(256)[None, :] // 64).astype(BF16)

    mk_all, mv_all = _memory_kv(mem_prompt, mem_norm[:, None, :], jnp.swapaxes(w_mem_kv, 1, 2).astype(BF16),
                                xa_knorm[:, :, None])

    lb_w = jax.nn.softmax(hg_lb.astype(F32), axis=0)
    lower_bounds = jnp.cumsum(lb_w, axis=0) - lb_w[0]

    outs = {}
    for l in range(depth):
        kind, j = l % 4, l // 4
        if kind == 0:
            wi = w_in_a[j]
            w_packed = jnp.concatenate([wi[:, :cc + w1024], wi[:, cc + w1024 + 16:], _pad_cols(wi[:, cc + w1024:cc + w1024 + 16], LANES)], axis=1)
            xq_blk = (cc + w1024) // XA_WIDTH
        elif kind == 1:
            w_packed = w_in_b[j]
            xq_blk = 4 * w1024 // XA_WIDTH
        elif kind == 2:
            wi = w_in_c[j]
            w_packed = jnp.concatenate([wi[:, :4 * w1024], wi[:, 4 * w1024 + 16:], _pad_cols(wi[:, 4 * w1024:4 * w1024 + 16], LANES)], axis=1)
            xq_blk = 4 * w1024 // XA_WIDTH
        else:
            w_packed = w_in_d[j]
            xq_blk = 2 * w1024 // XA_WIDTH
        w_bf = w_packed.astype(BF16)
        g_mix = norm_mix[l][None, :]
        pp_p = _norm_proj(h_p, g_mix, w_bf, tm=tm_p)
        pp_s = _norm_proj(h_s, g_mix, w_bf, tm=tm_s)

        qn_t = jnp.tile(xa_qnorm[l], XA_HEADS)[None, :]
        xo_p = _mem_attend(pp_p, mk_all[l], mv_all[l], qn_t, bd64, groups=bp, t=seq, tt=1024, gs=1, rg=128, col_blk=xq_blk)
        to_hdn = lambda a: jnp.transpose(a, (0, 2, 3, 1)).reshape(ds, XA_WIDTH, N_MEM)
        xo_s = _mem_attend(pp_s, to_hdn(cache_mem_k[l]), to_hdn(cache_mem_v[l]),
                           qn_t, bd64, groups=ds, t=dseq, tt=dseq, gs=4, rg=dseq, col_blk=xq_blk)

        if kind == 0:
            hp = jnp.concatenate([_lane_row(a_log[j], 8), _lane_row(a_dt_bias[j], 8)], axis=0)
            nw = a_norm_w[j][None, :]
            conv0_p = jnp.zeros((bp, 8, cc), F32)
            conv0_s = jnp.pad(state_a_conv[j], ((0, 0), (5, 0), (0, 0)))
            mo_p, st_p = _gdn(pp_p, conv0_p, jnp.zeros((bp, N_HEADS128, HEAD128, HEAD128), F32), a_conv_w[j], hp, nw,
                              groups=bp, t=seq, r=GDN_ROWS, c=GDN_CHUNK)
            mo_s, st_s = _gdn(pp_s, conv0_s, state_a_ssm[j], a_conv_w[j], hp, nw,
                              groups=ds, t=dseq, r=dseq, c=dseq)
            outs.setdefault("a_conv_p", []).append(pp_p.reshape(bp, seq, -1)[:, seq - 3:, :cc])
            outs.setdefault("a_conv_s", []).append(pp_s.reshape(ds, dseq, -1)[:, dseq - 3:, :cc])
            outs.setdefault("a_ssm_p", []).append(st_p)
            outs.setdefault("a_ssm_s", []).append(st_s)
        elif kind == 1:
            lb = lower_bounds[l][None, :]
            nw = b_norm_w[j][None, :]
            mo_p, st_p = _hgrn(pp_p, jnp.zeros((bp, N_HEADS128, HEAD128, HEAD128), F32), lb, nw,
                               groups=bp, t=seq, r=HG_ROWS, c=HG_CHUNK, row_off=0)
            mo_s, st_s = _hgrn(pp_s, state_b_ssm[j], lb, nw, groups=ds, t=dseq, r=4 * dseq, c=dseq, row_off=0, gs=4)
            outs.setdefault("b_ssm_p", []).append(st_p)
            outs.setdefault("b_ssm_s", []).append(st_s)
        elif kind == 2:
            qn_f = jnp.tile(c_qnorm[j], FOX_HEADS)[None, :]
            kn_f = jnp.tile(c_knorm[j], FOX_HEADS)[None, :]
            fb = _lane_row(c_fbias[j])
            q_p, kt_p, vt_p, lft_p, fc_p, kb_p = _fox_prep_t(pp_p, qn_f, kn_f, fb, bd64, groups=bp, t=seq, tr=256)
            q_s, k_s, lf_s, _ = _fox_prep(pp_s, qn_f, kn_f, fb, bd64, groups=ds, t=dseq, tr=dseq, row_off=0)
            mo_p = _fox_flash(q_p, kt_p, kb_p, pp_p, fc_p, groups=bp, t=seq, tq=FOX_TQ)
            pos_minor = lambda a: jnp.transpose(a, (0, 2, 3, 1))
            mo_s = _fox_sample(page_table, pos_minor(cache_c_k[j]), pos_minor(cache_c_v[j]),
                               jnp.swapaxes(cache_c_logf[j], 1, 2), q_s, k_s, pp_s, lf_s,
                               tnew=dseq, row_off=0, gp=PAGES_PER_STEP)
            time_major = lambda a: jnp.transpose(a.reshape(bp, FOX_HEADS, FOX_DIM, seq), (0, 3, 1, 2))
            outs.setdefault("c_k_p", []).append(time_major(kt_p))
            outs.setdefault("c_v_p", []).append(time_major(vt_p))
            outs.setdefault("c_lf_p", []).append(jnp.swapaxes(lft_p, 1, 2))
            outs.setdefault("c_k_s", []).append(k_s.reshape(ds, dseq, FOX_HEADS, FOX_DIM))
            outs.setdefault("c_v_s", []).append(pp_s[:, 2 * w1024:3 * w1024].reshape(ds, dseq, FOX_HEADS, FOX_DIM))
            outs.setdefault("c_lf_s", []).append(lf_s[:, :FOX_HEADS].reshape(ds, dseq, FOX_HEADS))
        else:
            gd = w1024 // CM_GROUPS
            lg, lbias = d_ln_g[j][None, :], d_ln_b[j][None, :]
            (mo_p,) = _chunk_mlp(pp_p, lg, lbias, d_ws[j], jnp.broadcast_to(d_bs[j][:, :, None], (CM_GROUPS, CM_CHUNK, gd)),
                                 nrows=np_rows, row_off=0, emit_v=False)
            mo_s, v_s = _chunk_mlp(pp_s, lg, lbias, d_ws[j][:, :dseq, :dseq],
                                   jnp.broadcast_to(d_bs[j][:, :dseq, None], (CM_GROUPS, dseq, gd)), nrows=ns_rows, row_off=0,
                                   emit_v=True)
            outs.setdefault("d_v_s", []).append(v_s.reshape(ds, dseq, w1024))

        wo = w_out[l].astype(BF16)
        mlp_w = (wo[:w1024], wo[w1024:], norm_mlp[l][None, :], w_up[l].astype(BF16), w_down[l].astype(BF16))
        h_p = _out_mlp(h_p, mo_p, xo_p, *mlp_w, tm=tm_p)
        h_s = _out_mlp(h_s, mo_s, xo_s, *mlp_w, tm=tm_s)

    st = lambda name: jnp.stack(outs[name])
    mem_out = lambda a: jnp.transpose(a.reshape(depth, bp, XA_HEADS, XA_DIM, N_MEM), (0, 1, 4, 2, 3))
    return (h_p.reshape(bp, seq, d), h_s.reshape(ds, dseq, d),
            st("a_conv_p"), st("a_conv_s"), st("a_ssm_p"), st("a_ssm_s"),
            st("b_ssm_p"), st("b_ssm_s"),
            st("c_k_p"), st("c_v_p"), st("c_lf_p"), st("c_k_s"), st("c_v_s"), st("c_lf_s"),
            st("d_v_s"), mem_out(mk_all), mem_out(mv_all))
```
